```python
import math
import jax
import jax.numpy as jnp
from jax import lax
import numpy as np

D_MODEL = 4096
BATCH = 4
SEQ = 2048
DEPTH = 4
DEC_BATCH = 32
DEC_SEQ = 32
PAST_LEN = 1024

CHUNK = 64
W_BRANCH = 2048
W_A = W_BRANCH
NB_A = 16
BW_A = W_A // NB_A
CONV_W = 4
LRU_C = 8.0
H_B = 16
HD_B = 128
N_KV = 4
H_I = 16
D_I = 64
TOPK_MAX = 256
IDX_W_SCALE = (H_I ** -0.5) * (D_I ** -0.5)
H_C = 4
DK_C = 256
DV_C = 512
GATE_RANK = 16
GATE_NORM = 16.0
GLA_BLOCK = 16
N_BRANCH = 3
D_FF = 11008
N_EXPERTS = 8
TOP_K = 2
D_FF_E = 7168
N_DENSE = (DEPTH + 1) // 2
N_MOE = DEPTH // 2
ALPHA = (2.0 * DEPTH) ** 0.25
BETA = (8.0 * DEPTH) ** -0.25
LN_EPS = 1e-5
RMS_EPS = 1e-6

SPLITS = (W_A, W_A, H_B * HD_B, N_KV * HD_B, N_KV * HD_B, H_I * D_I, D_I, H_I,
          H_C * DK_C, H_C * DK_C, H_C * DV_C, H_C * DV_C, GATE_RANK, N_BRANCH * D_MODEL)
N_IN = sum(SPLITS)
SPLIT_POINTS = tuple(sum(SPLITS[:i + 1]) for i in range(len(SPLITS) - 1))

kernel_name = 'hybrid_streaming_encoder_step'


def layer_norm(x, g, b):
    xf = x.astype(jnp.float32)
    mu = jnp.mean(xf, axis=-1, keepdims=True)
    var = jnp.mean(jnp.square(xf - mu), axis=-1, keepdims=True)
    return ((xf - mu) * lax.rsqrt(var + LN_EPS) * g + b).astype(x.dtype)


def causal_conv(x, buf, w, b):
    t_len = x.shape[1]
    xp = jnp.concatenate([buf.astype(x.dtype), x], axis=1)
    y = b
    for j in range(CONV_W):
        y = y + xp[:, j:j + t_len] * w[j]
    return y, xp[:, -(CONV_W - 1):]


def rg_lru(xc, h0, w_rg, b_rg, w_ig, b_ig, lam):
    bsz, t_len, _ = xc.shape
    xb = xc.reshape(bsz, t_len, NB_A, BW_A)
    r = jax.nn.sigmoid(jnp.einsum('btnd,nde->btne', xb, w_rg).reshape(bsz, t_len, W_A) + b_rg).astype(jnp.float32)
    i = jax.nn.sigmoid(jnp.einsum('btnd,nde->btne', xb, w_ig).reshape(bsz, t_len, W_A) + b_ig)
    log_a = -LRU_C * r * jax.nn.softplus(-lam.astype(jnp.float32))
    a = jnp.exp(log_a)
    u = jnp.sqrt(-jnp.expm1(2.0 * log_a)) * (i * xc).astype(jnp.float32)

    def combine(left, right):
        a1, b1 = left
        a2, b2 = right
        return a1 * a2, a2 * b1 + b2

    a_cum, b_cum = lax.associative_scan(combine, (a, u), axis=1)
    h = a_cum * h0.astype(jnp.float32)[:, None] + b_cum
    return h, h[:, -1]


def gla(q, k, v, log_g, s0):
    bsz, t_len, n_h, dk = q.shape
    dv = v.shape[-1]
    blk = math.gcd(GLA_BLOCK, t_len)
    n_blk = t_len // blk

    def to_blocks(t):
        return jnp.moveaxis(t.astype(jnp.float32).reshape((bsz, n_blk, blk) + t.shape[2:]), 1, 0)

    tri = jnp.tril(jnp.ones((blk, blk), dtype=bool))

    def step(s, inp):
        qc, kc, vc, gc = inp
        b = jnp.cumsum(gc, axis=1)
        qe = qc * jnp.exp(b)
        ke = kc * jnp.exp(-b)
        att = jnp.where(tri, jnp.einsum('bthd,bshd->bhts', qe, ke), 0.0)
        o = jnp.einsum('bhts,bshv->bthv', att, vc) + jnp.einsum('bthd,bhdv->bthv', qe, s)
        b_last = b[:, -1]
        s_new = jnp.exp(b_last)[..., None] * s + jnp.einsum('bshd,bshv->bhdv', kc * jnp.exp(b_last[:, None] - b), vc)
        return s_new, o

    s_fin, o = lax.scan(step, s0.astype(jnp.float32), (to_blocks(q), to_blocks(k), to_blocks(v), to_blocks(log_g)))
    o = jnp.moveaxis(o, 0, 1).reshape(bsz, t_len, n_h, dv)
    return o, s_fin


def dsa_block(q, qi, wi, k, v, ki, limit, n_sel):
    bsz, n_q = q.shape[:2]
    n_keys = k.shape[1]
    rel = jax.nn.relu(jnp.einsum('bqhd,bsd->bqhs', qi, ki))
    score = jnp.einsum('bqh,bqhs->bqs', wi, rel).astype(jnp.float32)
    score = jnp.where(jnp.arange(n_keys) < limit, score, -jnp.inf)
    top, idx = lax.top_k(score, n_sel)
    valid = top > -jnp.inf
    gather = jax.vmap(lambda t, i: t[i])
    k_sel = gather(k, idx)
    v_sel = gather(v, idx)
    qg = q.reshape(bsz, n_q, N_KV, H_B // N_KV, HD_B)
    s = jnp.einsum('bqngd,bqknd->bqngk', qg, k_sel).astype(jnp.float32) * (HD_B ** -0.5)
    s = jnp.where(valid[:, :, None, None, :], s, -jnp.inf)
    p = jax.nn.softmax(s, axis=-1)
    o = jnp.einsum('bqngk,bqknd->bqngd', p.astype(v_sel.dtype), v_sel)
    return o.reshape(bsz, n_q, H_B * HD_B)


def dsa_prompt(q, qi, wi, k, v, ki, n_sel):
    bsz, t_len = q.shape[:2]
    n_blk = t_len // CHUNK

    def blocks(t):
        return jnp.moveaxis(t.reshape((bsz, n_blk, CHUNK) + t.shape[2:]), 1, 0)

    def one(args):
        qb, qib, wib, j = args
        return dsa_block(qb, qib, wib, k, v, ki, (j + 1) * CHUNK, n_sel)

    o = lax.map(one, (blocks(q), blocks(qi), blocks(wi), jnp.arange(n_blk)))
    return jnp.moveaxis(o, 0, 1).reshape(bsz, t_len, H_B * HD_B)


def mixer(x, conv_buf, h0, s0, k_past, v_past, ki_past, p):
    bsz, t_len, _ = x.shape
    proj = x @ p['w_in']
    (xa, ga, qb, kb, vb, qi, ki, wi, qc, kc, vc, rc, gl, gz) = jnp.split(proj, SPLIT_POINTS, axis=-1)
    xconv, conv_new = causal_conv(xa, conv_buf, p['conv_w'], p['conv_b'])
    h, h_last = rg_lru(xconv, h0, p['w_rg'], p['b_rg'], p['w_ig'], p['b_ig'], p['lam'])
    out_a = h.astype(x.dtype) * jax.nn.gelu(ga)
    q = qb.reshape(bsz, t_len, H_B, HD_B)
    k = kb.reshape(bsz, t_len, N_KV, HD_B)
    v = vb.reshape(bsz, t_len, N_KV, HD_B)
    qi = qi.reshape(bsz, t_len, H_I, D_I)
    wi = wi * IDX_W_SCALE
    if k_past is None:
        out_b = dsa_prompt(q, qi, wi, k, v, ki, min(TOPK_MAX, t_len // 4))
    else:
        k_all = jnp.concatenate([k_past.astype(k.dtype), k], axis=1)
        v_all = jnp.concatenate([v_past.astype(v.dtype), v], axis=1)
        ki_all = jnp.concatenate([ki_past.astype(ki.dtype), ki], axis=1)
        n_keys = k_all.shape[1]
        out_b = dsa_block(q, qi, wi, k_all, v_all, ki_all, n_keys, min(TOPK_MAX, n_keys // 4))
    log_g = jax.nn.log_sigmoid((gl @ p['w_gg'] + p['b_gg']).astype(jnp.float32)) / GATE_NORM
    o_c, s_new = gla(qc.reshape(bsz, t_len, H_C, DK_C) * (DK_C ** -0.5),
                     kc.reshape(bsz, t_len, H_C, DK_C),
                     vc.reshape(bsz, t_len, H_C, DV_C),
                     log_g.reshape(bsz, t_len, H_C, DK_C), s0)
    o_c = o_c * lax.rsqrt(jnp.mean(o_c * o_c, axis=-1, keepdims=True) + RMS_EPS) * p['gla_g']
    out_c = o_c.reshape(bsz, t_len, H_C * DV_C).astype(x.dtype) * jax.nn.silu(rc)
    gates = jax.nn.sigmoid(gz + p['b_gate']).reshape(bsz, t_len, N_BRANCH, D_MODEL)
    branches = jnp.stack([out_a, out_b, out_c], axis=2)
    proj_br = jnp.einsum('btjw,jwd->btjd', branches, p['w_branch'])
    y = jnp.sum(gates * proj_br, axis=2) @ p['w_out']
    return y, (k, v, ki, h_last, conv_new, s_new)


def swiglu(x, wg, wu, wd):
    return (jax.nn.silu(x @ wg) * (x @ wu)) @ wd


def moe(x, w_router, wg, wu, wd):
    logits = (x @ w_router).astype(jnp.float32)
    top_l, idx = lax.top_k(logits, TOP_K)
    w = jax.nn.softmax(top_l, axis=-1)
    combine = jnp.sum(jax.nn.one_hot(idx, N_EXPERTS, dtype=jnp.float32) * w[..., None], axis=-2)
    out = jnp.zeros_like(x)
    for e in range(N_EXPERTS):
        out = out + combine[..., e:e + 1].astype(x.dtype) * swiglu(x, wg[e], wu[e], wd[e])
    return out


def setup_inputs(seed: int = 0) -> dict:
    key = jax.random.key(seed)
    ks = jax.random.split(key, 40)
    f32 = jnp.float32

    def nrm(k, shape, scale):
        return jax.random.normal(k, shape, f32) * scale

    a0 = jax.random.uniform(ks[10], (DEPTH, W_A), f32, 0.9, 0.999)
    s = a0 ** (1.0 / LRU_C)
    return {
        'x_prompt': nrm(ks[0], (BATCH, SEQ, D_MODEL), 1.0),
        'x_sample': nrm(ks[1], (DEC_BATCH, DEC_SEQ, D_MODEL), 1.0),
        'cache_k': nrm(ks[2], (DEPTH, DEC_BATCH, PAST_LEN, N_KV, HD_B), 1.0),
        'cache_v': nrm(ks[3], (DEPTH, DEC_BATCH, PAST_LEN, N_KV, HD_B), 1.0),
        'cache_kidx': nrm(ks[4], (DEPTH, DEC_BATCH, PAST_LEN, D_I), 1.0),
        'state_lru': nrm(ks[5], (DEPTH, DEC_BATCH, W_A), 0.5),
        'state_conv': nrm(ks[6], (DEPTH, DEC_BATCH, CONV_W - 1, W_A), 1.0),
        'state_gla': nrm(ks[7], (DEPTH, DEC_BATCH, H_C, DK_C, DV_C), 1.0),
        'w_in': nrm(ks[8], (DEPTH, D_MODEL, N_IN), D_MODEL ** -0.5),
        'conv_w': nrm(ks[9], (DEPTH, CONV_W, W_A), CONV_W ** -0.5),
        'conv_b': nrm(ks[11], (DEPTH, W_A), 0.01),
        'w_rec_gate': nrm(ks[12], (DEPTH, NB_A, BW_A, BW_A), BW_A ** -0.5),
        'b_rec_gate': nrm(ks[13], (DEPTH, W_A), 0.01),
        'w_in_gate': nrm(ks[14], (DEPTH, NB_A, BW_A, BW_A), BW_A ** -0.5),
        'b_in_gate': nrm(ks[15], (DEPTH, W_A), 0.01),
        'lru_lambda': jnp.log(s) - jnp.log1p(-s),
        'w_gla_gate': nrm(ks[16], (DEPTH, GATE_RANK, H_C * DK_C), GATE_RANK ** -0.5),
        'b_gla_gate': nrm(ks[17], (DEPTH, H_C * DK_C), 0.01),
        'gla_norm_g': 1.0 + nrm(ks[18], (DEPTH, DV_C), 0.01),
        'w_branch': nrm(ks[19], (DEPTH, N_BRANCH, W_BRANCH, D_MODEL), BETA * W_BRANCH ** -0.5),
        'b_branch_gate': nrm(ks[20], (DEPTH, N_BRANCH * D_MODEL), 0.01),
        'w_out': nrm(ks[21], (DEPTH, D_MODEL, D_MODEL), BETA * D_MODEL ** -0.5),
        'ln1_g': 1.0 + nrm(ks[22], (DEPTH, D_MODEL), 0.01),
        'ln1_b': nrm(ks[23], (DEPTH, D_MODEL), 0.01),
        'ln2_g': 1.0 + nrm(ks[24], (DEPTH, D_MODEL), 0.01),
        'ln2_b': nrm(ks[25], (DEPTH, D_MODEL), 0.01),
        'w_ff_gate': nrm(ks[26], (N_DENSE, D_MODEL, D_FF), D_MODEL ** -0.5),
        'w_ff_up': nrm(ks[27], (N_DENSE, D_MODEL, D_FF), D_MODEL ** -0.5),
        'w_ff_down': nrm(ks[28], (N_DENSE, D_FF, D_MODEL), BETA * D_FF ** -0.5),
        'w_router': nrm(ks[29], (N_MOE, D_MODEL, N_EXPERTS), D_MODEL ** -0.5),
        'w_exp_gate': nrm(ks[30], (N_MOE, N_EXPERTS, D_MODEL, D_FF_E), D_MODEL ** -0.5),
        'w_exp_up': nrm(ks[31], (N_MOE, N_EXPERTS, D_MODEL, D_FF_E), D_MODEL ** -0.5),
        'w_exp_down': nrm(ks[32], (N_MOE, N_EXPERTS, D_FF_E, D_MODEL), BETA * D_FF_E ** -0.5),
    }


def reference(x_prompt, x_sample, cache_k, cache_v, cache_kidx, state_lru, state_conv, state_gla,
              w_in, conv_w, conv_b, w_rec_gate, b_rec_gate, w_in_gate, b_in_gate, lru_lambda,
              w_gla_gate, b_gla_gate, gla_norm_g, w_branch, b_branch_gate, w_out,
              ln1_g, ln1_b, ln2_g, ln2_b, w_ff_gate, w_ff_up, w_ff_down,
              w_router, w_exp_gate, w_exp_up, w_exp_down):
    f32 = jnp.float32
    xp, xs = x_prompt, x_sample
    n_pb = xp.shape[0]
    outs_p = [[] for _ in range(6)]
    outs_s = [[] for _ in range(6)]
    for l in range(DEPTH):
        p = {'w_in': w_in[l], 'conv_w': conv_w[l], 'conv_b': conv_b[l],
             'w_rg': w_rec_gate[l], 'b_rg': b_rec_gate[l], 'w_ig': w_in_gate[l], 'b_ig': b_in_gate[l],
             'lam': lru_lambda[l], 'w_gg': w_gla_gate[l], 'b_gg': b_gla_gate[l], 'gla_g': gla_norm_g[l],
             'w_branch': w_branch[l], 'b_gate': b_branch_gate[l], 'w_out': w_out[l]}
        yp, st_p = mixer(xp, jnp.zeros((n_pb, CONV_W - 1, W_A), xp.dtype), jnp.zeros((n_pb, W_A), f32),
                         jnp.zeros((n_pb, H_C, DK_C, DV_C), f32), None, None, None, p)
        ysm, st_s = mixer(xs, state_conv[l], state_lru[l], state_gla[l], cache_k[l], cache_v[l], cache_kidx[l], p)
        for j in range(6):
            outs_p[j].append(st_p[j])
            outs_s[j].append(st_s[j])
        xp = layer_norm(ALPHA * xp + yp, ln1_g[l], ln1_b[l])
        xs = layer_norm(ALPHA * xs + ysm, ln1_g[l], ln1_b[l])
        i = l // 2
        if l % 2 == 0:
            fp = swiglu(xp, w_ff_gate[i], w_ff_up[i], w_ff_down[i])
            fs = swiglu(xs, w_ff_gate[i], w_ff_up[i], w_ff_down[i])
        else:
            fp = moe(xp, w_router[i], w_exp_gate[i], w_exp_up[i], w_exp_down[i])
            fs = moe(xs, w_router[i], w_exp_gate[i], w_exp_up[i], w_exp_down[i])
        xp = layer_norm(ALPHA * xp + fp, ln2_g[l], ln2_b[l])
        xs = layer_norm(ALPHA * xs + fs, ln2_g[l], ln2_b[l])
    k_p, v_p, ki_p, lru_p, conv_p, gla_p = [jnp.stack(o) for o in outs_p]
    k_s, v_s, ki_s, lru_s, conv_s, gla_s = [jnp.stack(o) for o in outs_s]
    return (xp, xs, k_p, v_p, ki_p, lru_p, conv_p, gla_p, k_s, v_s, ki_s, lru_s, conv_s, gla_s)
```

```python
import functools

import jax
import jax.numpy as jnp
from jax import lax
from jax.experimental import pallas as pl
from jax.experimental.pallas import tpu as pltpu

F32 = jnp.float32
BF16 = jnp.bfloat16
MXU_DTYPE = BF16
HIGHEST = lax.Precision.HIGHEST

D_MODEL = 4096
BATCH, SEQ = 4, 2048
DEPTH = 4
DEC_BATCH, DEC_SEQ = 32, 32
PAST_LEN = 1024
T_PROMPT = BATCH * SEQ
T_SAMPLE = DEC_BATCH * DEC_SEQ
T_ALL = T_PROMPT + T_SAMPLE
CHUNK = 64
W_BRANCH = 2048
W_A = W_BRANCH
NB_A = 16
BW_A = W_A // NB_A
CONV_W = 4
LRU_C = 8.0
H_B, HD_B, N_KV = 16, 128, 4
H_I, D_I = 16, 64
TOPK = 256
IDX_W_SCALE = (H_I ** -0.5) * (D_I ** -0.5)
H_C, DK_C, DV_C = 4, 256, 512
GATE_RANK = 16
GATE_NORM = 16.0
N_BRANCH = 3
D_FF = 11008
D_FF_PAD = 11264
N_EXPERTS = 8
D_FF_E = 7168
ALPHA = (2.0 * DEPTH) ** 0.25
LN_EPS = 1e-5
RMS_EPS = 1e-6
LANE = 128
INT_MIN = -2 ** 31

C_XA, C_GA, C_QB, C_KB, C_VB, C_QI = 0, 2048, 4096, 6144, 6656, 7168
C_QC, C_KC, C_VC, C_RC = 8192, 9216, 10240, 12288
C_KIA, C_KIB, C_WI, C_GL = 14336, 14464, 14592, 14720
C_GZ = 15360
N_PROJ = C_GZ + N_BRANCH * D_MODEL

VMEM_LIMIT = 56 * 1024 * 1024


def _params(sem):
    return pltpu.CompilerParams(dimension_semantics=sem, vmem_limit_bytes=VMEM_LIMIT)


def _sigmoid(x):
    return 1.0 / (1.0 + jnp.exp(-x))


def _log_sigmoid(x):
    return jnp.minimum(x, 0.0) - jnp.log(1.0 + jnp.exp(-jnp.abs(x)))


def _dot(a, b):
    return jnp.dot(a, b, preferred_element_type=F32)


def _dot_nt(a, b):
    return lax.dot_general(a, b, (((1,), (1,)), ((), ())), preferred_element_type=F32)


def _dot_tn(a, b):
    return lax.dot_general(a, b, (((0,), (0,)), ((), ())), preferred_element_type=F32)


def _mm_body(x_ref, w_ref, o_ref, acc_ref):
    k = pl.program_id(2)

    @pl.when(k == 0)
    def _init():
        acc_ref[...] = jnp.zeros_like(acc_ref)

    acc_ref[...] += _dot(x_ref[...].astype(MXU_DTYPE), w_ref[...].astype(MXU_DTYPE))

    @pl.when(k == pl.num_programs(2) - 1)
    def _fin():
        o_ref[...] = acc_ref[...].astype(o_ref.dtype)


def _matmul(x, w, lead, out_dtype, tm, tn, tk):
    m_dim, k_dim = x.shape
    n_dim = w.shape[-1]
    nl = len(lead)
    return pl.pallas_call(
        _mm_body,
        grid=(m_dim // tm, n_dim // tn, k_dim // tk),
        in_specs=[pl.BlockSpec((tm, tk), lambda m, n, k: (m, k)),
                  pl.BlockSpec((None,) * nl + (tk, tn), lambda m, n, k: lead + (k, n))],
        out_specs=pl.BlockSpec((tm, tn), lambda m, n, k: (m, n)),
        out_shape=jax.ShapeDtypeStruct((m_dim, n_dim), out_dtype),
        scratch_shapes=[pltpu.VMEM((tm, tn), F32)],
        compiler_params=_params(("parallel", "parallel", "arbitrary")),
    )(x, w)


def _ln_body(x_ref, y_ref, g_ref, b_ref, o_ref, ob_ref):
    s = ALPHA * x_ref[...] + y_ref[...]
    mu = jnp.mean(s, axis=-1, keepdims=True)
    d = s - mu
    var = jnp.mean(d * d, axis=-1, keepdims=True)
    o = d * lax.rsqrt(var + LN_EPS) * g_ref[...] + b_ref[...]
    o_ref[...] = o
    ob_ref[...] = o.astype(BF16)


def _ln_residual(x, y, g, b, tm=256):
    t_dim = x.shape[0]
    row = pl.BlockSpec((tm, D_MODEL), lambda m: (m, 0))
    vec = pl.BlockSpec((1, D_MODEL), lambda m: (0, 0))
    return pl.pallas_call(
        _ln_body,
        grid=(t_dim // tm,),
        in_specs=[row, row, vec, vec],
        out_specs=[row, row],
        out_shape=[jax.ShapeDtypeStruct((t_dim, D_MODEL), F32), jax.ShapeDtypeStruct((t_dim, D_MODEL), BF16)],
        compiler_params=_params(("parallel",)),
    )(x, y, g.reshape(1, D_MODEL), b.reshape(1, D_MODEL))


def _glu_body(*refs, k_axis, expert_axis):
    if expert_axis is None:
        x_ref, wg_ref, wu_ref, o_ref, accg_ref, accu_ref = refs
    else:
        x_ref, wg_ref, wu_ref, c_ref, o_ref, accg_ref, accu_ref = refs
    k = pl.program_id(k_axis)

    @pl.when(k == 0)
    def _init():
        accg_ref[...] = jnp.zeros_like(accg_ref)
        accu_ref[...] = jnp.zeros_like(accu_ref)

    x = x_ref[...]
    accg_ref[...] += _dot(x, wg_ref[...].astype(MXU_DTYPE))
    accu_ref[...] += _dot(x, wu_ref[...].astype(MXU_DTYPE))

    @pl.when(k == pl.num_programs(k_axis) - 1)
    def _fin():
        g = accg_ref[...]
        h = g * _sigmoid(g) * accu_ref[...]
        if expert_axis is not None:
            e = pl.program_id(expert_axis)
            c = c_ref[...]
            lane = lax.broadcasted_iota(jnp.int32, c.shape, 1)
            h = h * jnp.sum(jnp.where(lane == e, c, 0.0), axis=1, keepdims=True)
        o_ref[...] = h.astype(o_ref.dtype)


def _glu_dense(xb, wg, wu, layer, tm=1024, tn=1024, tk=512):
    t_dim, k_dim = xb.shape
    n_dim = wg.shape[-1]
    wspec = pl.BlockSpec((None, tk, tn), lambda m, n, k: (layer, k, n))
    return pl.pallas_call(
        functools.partial(_glu_body, k_axis=2, expert_axis=None),
        grid=(t_dim // tm, n_dim // tn, k_dim // tk),
        in_specs=[pl.BlockSpec((tm, tk), lambda m, n, k: (m, k)), wspec, wspec],
        out_specs=pl.BlockSpec((tm, tn), lambda m, n, k: (m, n)),
        out_shape=jax.ShapeDtypeStruct((t_dim, n_dim), BF16),
        scratch_shapes=[pltpu.VMEM((tm, tn), F32), pltpu.VMEM((tm, tn), F32)],
        compiler_params=_params(("parallel", "parallel", "arbitrary")),
    )(xb, wg, wu)


def _glu_experts(xb, wg, wu, combine, layer, tm=1024, tn=1024, tk=512):
    t_dim, k_dim = xb.shape
    nt = D_FF_E // tn
    wspec = pl.BlockSpec((None, None, tk, tn), lambda m, e, n, k: (layer, e, k, n))
    return pl.pallas_call(
        functools.partial(_glu_body, k_axis=3, expert_axis=1),
        grid=(t_dim // tm, N_EXPERTS, nt, k_dim // tk),
        in_specs=[pl.BlockSpec((tm, tk), lambda m, e, n, k: (m, k)), wspec, wspec,
                  pl.BlockSpec((tm, LANE), lambda m, e, n, k: (m, 0))],
        out_specs=pl.BlockSpec((tm, tn), lambda m, e, n, k: (m, e * nt + n)),
        out_shape=jax.ShapeDtypeStruct((t_dim, N_EXPERTS * D_FF_E), BF16),
        scratch_shapes=[pltpu.VMEM((tm, tn), F32), pltpu.VMEM((tm, tn), F32)],
        compiler_params=_params(("parallel", "parallel", "parallel", "arbitrary")),
    )(xb, wg, wu, combine)


def _router_body(x_ref, w_ref, c_ref):
    logits = jnp.dot(x_ref[...], w_ref[...], preferred_element_type=F32, precision=HIGHEST)
    lane = lax.broadcasted_iota(jnp.int32, logits.shape, 1)
    logits = jnp.where(lane < N_EXPERTS, logits, -jnp.inf)
    m1 = jnp.max(logits, axis=1, keepdims=True)
    i1 = jnp.min(jnp.where(logits == m1, lane, LANE), axis=1, keepdims=True)
    rest = jnp.where(lane == i1, -jnp.inf, logits)
    m2 = jnp.max(rest, axis=1, keepdims=True)
    i2 = jnp.min(jnp.where(rest == m2, lane, LANE), axis=1, keepdims=True)
    e2 = jnp.exp(m2 - m1)
    w1 = 1.0 / (1.0 + e2)
    w2 = e2 / (1.0 + e2)
    c_ref[...] = jnp.where(lane == i1, w1, 0.0) + jnp.where(lane == i2, w2, 0.0)


def _router(x, w_pad, tm=512):
    t_dim = x.shape[0]
    return pl.pallas_call(
        _router_body,
        grid=(t_dim // tm,),
        in_specs=[pl.BlockSpec((tm, D_MODEL), lambda m: (m, 0)),
                  pl.BlockSpec((D_MODEL, LANE), lambda m: (0, 0))],
        out_specs=pl.BlockSpec((tm, LANE), lambda m: (m, 0)),
        out_shape=jax.ShapeDtypeStruct((t_dim, LANE), F32),
        compiler_params=_params(("parallel",)),
    )(x, w_pad)


def _merge_body(br_ref, w_ref, gz_ref, bg_ref, o_ref, acc_ref):
    j = pl.program_id(2)

    @pl.when(j == 0)
    def _init():
        acc_ref[...] = jnp.zeros_like(acc_ref)

    pj = _dot(br_ref[...].astype(MXU_DTYPE), w_ref[...].astype(MXU_DTYPE))
    acc_ref[...] += _sigmoid(gz_ref[...] + bg_ref[...]) * pj

    @pl.when(j == N_BRANCH - 1)
    def _fin():
        o_ref[...] = acc_ref[...].astype(o_ref.dtype)


def _merge(branches, w_branch, proj, b_gate, layer, tm=1024, tn=1024):
    nt = D_MODEL // tn
    gz_blk = C_GZ // tn
    return pl.pallas_call(
        _merge_body,
        grid=(T_ALL // tm, nt, N_BRANCH),
        in_specs=[pl.BlockSpec((None, tm, W_BRANCH), lambda m, n, j: (j, m, 0)),
                  pl.BlockSpec((None, None, W_BRANCH, tn), lambda m, n, j: (layer, j, 0, n)),
                  pl.BlockSpec((tm, tn), lambda m, n, j: (m, gz_blk + j * nt + n)),
                  pl.BlockSpec((1, tn), lambda m, n, j: (0, j * nt + n))],
        out_specs=pl.BlockSpec((tm, tn), lambda m, n, j: (m, n)),
        out_shape=jax.ShapeDtypeStruct((T_ALL, D_MODEL), BF16),
        scratch_shapes=[pltpu.VMEM((tm, tn), F32)],
        compiler_params=_params(("parallel", "parallel", "arbitrary")),
    )(branches, w_branch, proj, b_gate.reshape(1, N_BRANCH * D_MODEL))


def _lru_body(xa_ref, ga_ref, buf_ref, h0_ref, cw_ref, cb_ref, wrg_ref, brg_ref, wig_ref, big_ref, lam_ref,
              o_ref, hlast_ref, xp_ref, a_ref, u_ref, h_ref, *, tl):
    t = pl.program_id(1)

    @pl.when(t == 0)
    def _init():
        xp_ref[5:8, :] = buf_ref[...]
        h_ref[...] = h0_ref[...]

    xp_ref[8:8 + tl, :] = xa_ref[...]
    xc = cb_ref[...] + xp_ref[8:8 + tl, :] * cw_ref[3:4, :]
    for j in range(CONV_W - 1):
        xc = xc + xp_ref[5 + j:5 + j + tl, :] * cw_ref[j:j + 1, :]
    xp_ref[5:8, :] = xa_ref[tl - 3:tl, :]

    xcb = xc.astype(MXU_DTYPE)
    for n in range(NB_A):
        sl = slice(n * BW_A, (n + 1) * BW_A)
        xs = xcb[:, sl]
        r = _sigmoid(_dot(xs, wrg_ref[n].astype(MXU_DTYPE)) + brg_ref[:, sl])
        i = _sigmoid(_dot(xs, wig_ref[n].astype(MXU_DTYPE)) + big_ref[:, sl])
        lam = lam_ref[:, sl]
        softplus = jnp.maximum(-lam, 0.0) + jnp.log(1.0 + jnp.exp(-jnp.abs(lam)))
        log_a = -LRU_C * r * softplus
        a_ref[:, sl] = jnp.exp(log_a)
        u_ref[:, sl] = jnp.sqrt(1.0 - jnp.exp(2.0 * log_a)) * (i * xc[:, sl])

    def step(s, h):
        h = a_ref[pl.ds(s, 1), :] * h + u_ref[pl.ds(s, 1), :]
        a_ref[pl.ds(s, 1), :] = h
        return h

    h = lax.fori_loop(0, tl, step, h_ref[...], unroll=8)
    h_ref[...] = h
    hlast_ref[...] = h
    g = ga_ref[...]
    gelu = 0.5 * g * (1.0 + jnp.tanh(0.7978845608028654 * (g + 0.044715 * (g * g * g))))
    o_ref[...] = (a_ref[...] * gelu).astype(o_ref.dtype)


def _branch_a(proj, row0, nb, seq, tl, conv_buf, h0, cw, cb, wrg, brg, wig, big, lam):
    nt = seq // tl
    rb0 = row0 // tl
    vec = pl.BlockSpec((1, W_A), lambda b, t: (0, 0))
    blk = pl.BlockSpec((NB_A, BW_A, BW_A), lambda b, t: (0, 0, 0))
    out, h_last = pl.pallas_call(
        functools.partial(_lru_body, tl=tl),
        grid=(nb, nt),
        in_specs=[pl.BlockSpec((tl, W_A), lambda b, t: (rb0 + b * nt + t, C_XA // W_A)),
                  pl.BlockSpec((tl, W_A), lambda b, t: (rb0 + b * nt + t, C_GA // W_A)),
                  pl.BlockSpec((None, CONV_W - 1, W_A), lambda b, t: (b, 0, 0)),
                  pl.BlockSpec((None, 1, W_A), lambda b, t: (b, 0, 0)),
                  pl.BlockSpec((CONV_W, W_A), lambda b, t: (0, 0)),
                  vec, blk, vec, blk, vec, vec],
        out_specs=[pl.BlockSpec((tl, W_A), lambda b, t: (b * nt + t, 0)),
                   pl.BlockSpec((None, 1, W_A), lambda b, t: (b, 0, 0))],
        out_shape=[jax.ShapeDtypeStruct((nb * seq, W_A), BF16), jax.ShapeDtypeStruct((nb, 1, W_A), F32)],
        scratch_shapes=[pltpu.VMEM((tl + 8, W_A), F32), pltpu.VMEM((tl, W_A), F32), pltpu.VMEM((tl, W_A), F32),
                        pltpu.VMEM((1, W_A), F32)],
        compiler_params=_params(("parallel", "arbitrary")),
    )(proj, proj, conv_buf, h0.reshape(nb, 1, W_A), cw, cb.reshape(1, W_A), wrg, brg.reshape(1, W_A),
      wig, big.reshape(1, W_A), lam.reshape(1, W_A))
    return out, h_last.reshape(nb, W_A)


def _gla_body(q_ref, k_ref, v_ref, rc_ref, gl_ref, wgg_ref, wggt_ref, bgr_ref, bgc_ref, g_ref, s0_ref,
              o_ref, s_ref, *, ck):
    c = pl.program_id(2)

    @pl.when(c == 0)
    def _init():
        s_ref[...] = s0_ref[...]

    gl = gl_ref[...].astype(MXU_DTYPE)
    lg = _log_sigmoid(_dot(gl, wgg_ref[...].astype(MXU_DTYPE)) + bgr_ref[...]) / GATE_NORM
    lg_t = _log_sigmoid(_dot_nt(wggt_ref[...].astype(MXU_DTYPE), gl) + bgc_ref[...]) / GATE_NORM
    row = lax.broadcasted_iota(jnp.int32, (ck, ck), 0)
    col = lax.broadcasted_iota(jnp.int32, (ck, ck), 1)
    tri = row >= col
    bcum = jnp.dot(tri.astype(F32), lg, preferred_element_type=F32, precision=HIGHEST)
    b_last = bcum[ck - 1:ck, :]
    b_last_col = jnp.sum(lg_t, axis=1, keepdims=True)

    k = k_ref[...]
    qe = (q_ref[...] * (DK_C ** -0.5) * jnp.exp(bcum)).astype(MXU_DTYPE)
    ke = (k * jnp.exp(-bcum)).astype(MXU_DTYPE)
    kt = (k * jnp.exp(b_last - bcum)).astype(MXU_DTYPE)
    vb = v_ref[...].astype(MXU_DTYPE)
    att = jnp.where(tri, _dot_nt(qe, ke), 0.0)
    s = s_ref[...]
    o = _dot(att.astype(MXU_DTYPE), vb) + _dot(qe, s.astype(MXU_DTYPE))
    s_ref[...] = jnp.exp(b_last_col) * s + _dot_tn(kt, vb)

    o = o * lax.rsqrt(jnp.mean(o * o, axis=-1, keepdims=True) + RMS_EPS) * g_ref[...]
    rc = rc_ref[...]
    o_ref[...] = (o * (rc * _sigmoid(rc))).astype(o_ref.dtype)


def _branch_c(proj, row0, nb, seq, ck, s0, wgg_pad, wggt_pad, bgg, gla_g):
    nc = seq // ck
    rb0 = row0 // ck

    def rows(cb):
        return lambda b, h, c: (rb0 + b * nc + c, cb)

    out, s_fin = pl.pallas_call(
        functools.partial(_gla_body, ck=ck),
        grid=(nb, H_C, nc),
        in_specs=[pl.BlockSpec((ck, DK_C), lambda b, h, c: (rb0 + b * nc + c, C_QC // DK_C + h)),
                  pl.BlockSpec((ck, DK_C), lambda b, h, c: (rb0 + b * nc + c, C_KC // DK_C + h)),
                  pl.BlockSpec((ck, DV_C), lambda b, h, c: (rb0 + b * nc + c, C_VC // DV_C + h)),
                  pl.BlockSpec((ck, DV_C), lambda b, h, c: (rb0 + b * nc + c, C_RC // DV_C + h)),
                  pl.BlockSpec((ck, LANE), lambda b, h, c: (rb0 + b * nc + c, C_GL // LANE)),
                  pl.BlockSpec((LANE, DK_C), lambda b, h, c: (0, h)),
                  pl.BlockSpec((DK_C, LANE), lambda b, h, c: (h, 0)),
                  pl.BlockSpec((1, DK_C), lambda b, h, c: (0, h)),
                  pl.BlockSpec((DK_C, 1), lambda b, h, c: (h, 0)),
                  pl.BlockSpec((1, DV_C), lambda b, h, c: (0, 0)),
                  pl.BlockSpec((None, None, DK_C, DV_C), lambda b, h, c: (b, h, 0, 0))],
        out_specs=[pl.BlockSpec((ck, DV_C), lambda b, h, c: (b * nc + c, h)),
                   pl.BlockSpec((None, None, DK_C, DV_C), lambda b, h, c: (b, h, 0, 0))],
        out_shape=[jax.ShapeDtypeStruct((nb * seq, H_C * DV_C), BF16),
                   jax.ShapeDtypeStruct((nb, H_C, DK_C, DV_C), F32)],
        compiler_params=_params(("parallel", "parallel", "arbitrary")),
    )(proj, proj, proj, proj, proj, wgg_pad, wggt_pad, bgg.reshape(1, H_C * DK_C), bgg.reshape(H_C * DK_C, 1),
      gla_g.reshape(1, DV_C), s0)
    return out, s_fin


def _dsa_body(*refs, tq, lc, lk, lpad, chunked):
    if lc:
        (q_ref, qi_ref, wi_ref, kn_ref, vn_ref, kan_ref, kbn_ref, kc_ref, vc_ref, kac_ref, kbc_ref,
         o_ref, k_s, v_s, ka_s, kb_s, key_s) = refs
    else:
        (q_ref, qi_ref, wi_ref, kn_ref, vn_ref, kan_ref, kbn_ref,
         o_ref, k_s, v_s, ka_s, kb_s, key_s) = refs
    qt = pl.program_id(1)
    n_keys = lc + lk

    @pl.when(qt == 0)
    def _stage_keys():
        if lc:
            k_s[0:lc, :] = kc_ref[...].astype(MXU_DTYPE)
            v_s[0:lc, :] = vc_ref[...].astype(MXU_DTYPE)
            ka_s[0:lc, :] = kac_ref[...].astype(MXU_DTYPE)
            kb_s[0:lc, :] = kbc_ref[...].astype(MXU_DTYPE)
        k_s[lc:n_keys, :] = kn_ref[...].astype(MXU_DTYPE)
        v_s[lc:n_keys, :] = vn_ref[...].astype(MXU_DTYPE)
        ka_s[lc:n_keys, :] = kan_ref[...].astype(MXU_DTYPE)
        kb_s[lc:n_keys, :] = kbn_ref[...].astype(MXU_DTYPE)
        if lpad > n_keys:
            k_s[n_keys:lpad, :] = jnp.zeros((lpad - n_keys, N_KV * HD_B), MXU_DTYPE)
            v_s[n_keys:lpad, :] = jnp.zeros((lpad - n_keys, N_KV * HD_B), MXU_DTYPE)
            ka_s[n_keys:lpad, :] = jnp.zeros((lpad - n_keys, LANE), MXU_DTYPE)
            kb_s[n_keys:lpad, :] = jnp.zeros((lpad - n_keys, LANE), MXU_DTYPE)

    qi = qi_ref[...].astype(MXU_DTYPE)
    wi = wi_ref[...] * IDX_W_SCALE
    ka = ka_s[...]
    kb = kb_s[...]
    score = jnp.zeros((tq, lpad), F32)
    for p in range(H_I // 2):
        qp = qi[:, p * LANE:(p + 1) * LANE]
        score = score + wi[:, 2 * p:2 * p + 1] * jnp.maximum(_dot_nt(qp, ka), 0.0)
        score = score + wi[:, 2 * p + 1:2 * p + 2] * jnp.maximum(_dot_nt(qp, kb), 0.0)

    score = jnp.where(score == 0.0, 0.0, score)
    bits = lax.bitcast_convert_type(score, jnp.int32)
    key = bits ^ ((bits >> 31) & 0x7FFFFFFF)
    col = lax.broadcasted_iota(jnp.int32, (tq, lpad), 1)
    if chunked:
        pos = qt * tq + lax.broadcasted_iota(jnp.int32, (tq, lpad), 0)
        valid = col < (pos // CHUNK + 1) * CHUNK
    else:
        valid = col < n_keys
    key_s[...] = jnp.where(valid, key, INT_MIN)

    def count_ge(cand):
        return jnp.sum((key_s[...] >= cand).astype(jnp.int32), axis=1, keepdims=True)

    prefix = jnp.where(count_ge(jnp.zeros((tq, 1), jnp.int32)) >= TOPK, 0, INT_MIN).astype(jnp.int32)

    def search(i, prefix):
        cand = prefix | jnp.left_shift(jnp.int32(1), 30 - i)
        return jnp.where(count_ge(cand) >= TOPK, cand, prefix)

    prefix = lax.fori_loop(0, 31, search, prefix)
    bias = jnp.where((key_s[...] >= prefix) & valid, 0.0, -jnp.inf)

    q = q_ref[...]
    for n in range(N_KV):
        kn = k_s[:, n * HD_B:(n + 1) * HD_B]
        vn = v_s[:, n * HD_B:(n + 1) * HD_B]
        for g in range(H_B // N_KV):
            sl = slice((n * (H_B // N_KV) + g) * HD_B, (n * (H_B // N_KV) + g + 1) * HD_B)
            s = _dot_nt(q[:, sl].astype(MXU_DTYPE), kn) * (HD_B ** -0.5) + bias
            m = jnp.max(s, axis=1, keepdims=True)
            p = jnp.exp(s - m)
            den = jnp.sum(p, axis=1, keepdims=True)
            o_ref[:, sl] = (_dot(p.astype(MXU_DTYPE), vn) / den).astype(o_ref.dtype)


def _branch_b(proj, row0, nb, seq, tq, chunked, cache=None):
    nq = seq // tq
    rq0 = row0 // tq
    rk0 = row0 // seq
    lc = 0 if cache is None else cache[0].shape[1]
    lpad = -(-(lc + seq) // LANE) * LANE
    hk = N_KV * HD_B

    def qrows(cb):
        return lambda b, t: (rq0 + b * nq + t, cb)

    def krows(cb):
        return lambda b, t: (rk0 + b, cb)

    in_specs = [pl.BlockSpec((tq, H_B * HD_B), qrows(C_QB // (H_B * HD_B))),
                pl.BlockSpec((tq, H_I * D_I), qrows(C_QI // (H_I * D_I))),
                pl.BlockSpec((tq, LANE), qrows(C_WI // LANE)),
                pl.BlockSpec((seq, hk), krows(C_KB // hk)),
                pl.BlockSpec((seq, hk), krows(C_VB // hk)),
                pl.BlockSpec((seq, LANE), krows(C_KIA // LANE)),
                pl.BlockSpec((seq, LANE), krows(C_KIB // LANE))]
    args = [proj] * 7
    if cache is not None:
        in_specs += [pl.BlockSpec((None, lc, hk), lambda b, t: (b, 0, 0)),
                     pl.BlockSpec((None, lc, hk), lambda b, t: (b, 0, 0)),
                     pl.BlockSpec((None, lc, LANE), lambda b, t: (b, 0, 0)),
                     pl.BlockSpec((None, lc, LANE), lambda b, t: (b, 0, 0))]
        args += list(cache)
    return pl.pallas_call(
        functools.partial(_dsa_body, tq=tq, lc=lc, lk=seq, lpad=lpad, chunked=chunked),
        grid=(nb, nq),
        in_specs=in_specs,
        out_specs=pl.BlockSpec((tq, H_B * HD_B), lambda b, t: (b * nq + t, 0)),
        out_shape=jax.ShapeDtypeStruct((nb * seq, H_B * HD_B), BF16),
        scratch_shapes=[pltpu.VMEM((lpad, hk), MXU_DTYPE), pltpu.VMEM((lpad, hk), MXU_DTYPE),
                        pltpu.VMEM((lpad, LANE), MXU_DTYPE), pltpu.VMEM((lpad, LANE), MXU_DTYPE),
                        pltpu.VMEM((tq, lpad), jnp.int32)],
        compiler_params=_params(("parallel", "arbitrary")),
    )(*args)


def _relayout_w_in(w_in):
    def z(n):
        return jnp.zeros(w_in.shape[:2] + (n,), w_in.dtype)

    ki = w_in[..., 8192:8256]
    wi = w_in[..., 8256:8272]
    gl = w_in[..., 14416:14432]
    parts = [w_in[..., :8192], w_in[..., 8272:14416],
             ki, z(LANE - D_I), z(LANE - D_I), ki, wi, z(LANE - H_I), gl, z(LANE - GATE_RANK),
             z(C_GZ - C_GL - LANE), w_in[..., 14432:]]
    return jnp.concatenate(parts, axis=-1).astype(BF16)


def kernel(x_prompt, x_sample, cache_k, cache_v, cache_kidx, state_lru, state_conv, state_gla, w_in, conv_w, conv_b, w_rec_gate, b_rec_gate, w_in_gate, b_in_gate, lru_lambda, w_gla_gate, b_gla_gate, gla_norm_g, w_branch, b_branch_gate, w_out, ln1_g, ln1_b, ln2_g, ln2_b, w_ff_gate, w_ff_up, w_ff_down, w_router, w_exp_gate, w_exp_up, w_exp_down):
    hk = N_KV * HD_B
    x = jnp.concatenate([x_prompt.reshape(T_PROMPT, D_MODEL), x_sample.reshape(T_SAMPLE, D_MODEL)], axis=0)
    xb = x.astype(BF16)

    w_in_p = _relayout_w_in(w_in)
    ff_pad = D_FF_PAD - D_FF
    w_ffg = jnp.pad(w_ff_gate, ((0, 0), (0, 0), (0, ff_pad))).astype(BF16)
    w_ffu = jnp.pad(w_ff_up, ((0, 0), (0, 0), (0, ff_pad))).astype(BF16)
    w_ffd = jnp.pad(w_ff_down, ((0, 0), (0, ff_pad), (0, 0))).astype(BF16)
    w_router_p = jnp.pad(w_router, ((0, 0), (0, 0), (0, LANE - N_EXPERTS)))
    w_expd = w_exp_down.reshape(DEPTH // 2, N_EXPERTS * D_FF_E, D_MODEL)
    wgg_p = jnp.pad(w_gla_gate, ((0, 0), (0, LANE - GATE_RANK), (0, 0)))
    wggt_p = jnp.swapaxes(wgg_p, 1, 2)
    cache_k2 = cache_k.reshape(DEPTH, DEC_BATCH, PAST_LEN, hk)
    cache_v2 = cache_v.reshape(DEPTH, DEC_BATCH, PAST_LEN, hk)
    cache_kia = jnp.pad(cache_kidx, ((0, 0), (0, 0), (0, 0), (0, LANE - D_I)))
    cache_kib = jnp.pad(cache_kidx, ((0, 0), (0, 0), (0, 0), (LANE - D_I, 0)))
    zeros_conv = jnp.zeros((BATCH, CONV_W - 1, W_A), F32)
    zeros_lru = jnp.zeros((BATCH, W_A), F32)
    zeros_gla = jnp.zeros((BATCH, H_C, DK_C, DV_C), F32)

    outs_p = [[] for _ in range(6)]
    outs_s = [[] for _ in range(6)]
    for l in range(DEPTH):
        proj = _matmul(xb, w_in_p, (l,), F32, tm=1536, tn=1024, tk=1024)

        lru_args = (conv_w[l], conv_b[l], w_rec_gate[l], b_rec_gate[l], w_in_gate[l], b_in_gate[l], lru_lambda[l])
        a_p, lru_p = _branch_a(proj, 0, BATCH, SEQ, 256, zeros_conv, zeros_lru, *lru_args)
        a_s, lru_s = _branch_a(proj, T_PROMPT, DEC_BATCH, DEC_SEQ, DEC_SEQ, state_conv[l], state_lru[l], *lru_args)

        b_p = _branch_b(proj, 0, BATCH, SEQ, 128, True)
        b_s = _branch_b(proj, T_PROMPT, DEC_BATCH, DEC_SEQ, DEC_SEQ, False,
                        cache=(cache_k2[l], cache_v2[l], cache_kia[l], cache_kib[l]))

        gla_args = (wgg_p[l], wggt_p[l], b_gla_gate[l], gla_norm_g[l])
        c_p, gla_p = _branch_c(proj, 0, BATCH, SEQ, 64, zeros_gla, *gla_args)
        c_s, gla_s = _branch_c(proj, T_PROMPT, DEC_BATCH, DEC_SEQ, DEC_SEQ, state_gla[l], *gla_args)

        branches = jnp.stack([jnp.concatenate([a_p, a_s], axis=0), jnp.concatenate([b_p, b_s], axis=0),
                              jnp.concatenate([c_p, c_s], axis=0)])
        mixed = _merge(branches, w_branch, proj, b_branch_gate[l], l)
        y = _matmul(mixed, w_out, (l,), F32, tm=1536, tn=1024, tk=512)
        x, xb = _ln_residual(x, y, ln1_g[l], ln1_b[l])

        i = l // 2
        if l % 2 == 0:
            h = _glu_dense(xb, w_ffg, w_ffu, i)
            f = _matmul(h, w_ffd, (i,), F32, tm=1536, tn=1024, tk=1024)
        else:
            combine = _router(x, w_router_p[i])
            h = _glu_experts(xb, w_exp_gate, w_exp_up, combine, i)
            f = _matmul(h, w_expd, (i,), F32, tm=1536, tn=1024, tk=512)
        x, xb = _ln_residual(x, f, ln2_g[l], ln2_b[l])

        pp, ps = proj[:T_PROMPT], proj[T_PROMPT:]
        for dst, rows, nb, seq, lru_h, gla_st in ((outs_p, pp, BATCH, SEQ, lru_p, gla_p),
                                                  (outs_s, ps, DEC_BATCH, DEC_SEQ, lru_s, gla_s)):
            dst[0].append(rows[:, C_KB:C_KB + hk].reshape(nb, seq, N_KV, HD_B))
            dst[1].append(rows[:, C_VB:C_VB + hk].reshape(nb, seq, N_KV, HD_B))
            dst[2].append(rows[:, C_KIA:C_KIA + D_I].reshape(nb, seq, D_I))
            dst[3].append(lru_h)
            dst[4].append(rows[:, C_XA:C_XA + W_A].reshape(nb, seq, W_A)[:, seq - (CONV_W - 1):])
            dst[5].append(gla_st)

    k_p, v_p, ki_p, lru_po, conv_p, gla_po = [jnp.stack(o) for o in outs_p]
    k_s, v_s, ki_s, lru_so, conv_s, gla_so = [jnp.stack(o) for o in outs_s]
    return (x[:T_PROMPT].reshape(BATCH, SEQ, D_MODEL), x[T_PROMPT:].reshape(DEC_BATCH, DEC_SEQ, D_MODEL),
            k_p, v_p, ki_p, lru_po, conv_p, gla_po, k_s, v_s, ki_s, lru_so, conv_s, gla_so)
```

```python
import functools

import jax
import jax.numpy as jnp
from jax import lax
from jax.experimental import pallas as pl
from jax.experimental.pallas import tpu as pltpu

F32 = jnp.float32
BF16 = jnp.bfloat16
MXU_DTYPE = BF16
HIGHEST = lax.Precision.HIGHEST

D_MODEL = 4096
BATCH, SEQ = 4, 2048
DEPTH = 4
DEC_BATCH, DEC_SEQ = 32, 32
PAST_LEN = 1024
T_PROMPT = BATCH * SEQ
T_SAMPLE = DEC_BATCH * DEC_SEQ
T_ALL = T_PROMPT + T_SAMPLE
CHUNK = 64
W_BRANCH = 2048
W_A = W_BRANCH
NB_A = 16
BW_A = W_A // NB_A
CONV_W = 4
LRU_C = 8.0
H_B, HD_B, N_KV = 16, 128, 4
H_I, D_I = 16, 64
TOPK = 256
IDX_W_SCALE = (H_I ** -0.5) * (D_I ** -0.5)
H_C, DK_C, DV_C = 4, 256, 512
GATE_RANK = 16
GATE_NORM = 16.0
N_BRANCH = 3
D_FF = 11008
D_FF_PAD = 11264
N_EXPERTS = 8
D_FF_E = 7168
ALPHA = (2.0 * DEPTH) ** 0.25
LN_EPS = 1e-5
RMS_EPS = 1e-6
LANE = 128
INT_MIN = -2 ** 31

C_XA, C_GA, C_QB, C_KB, C_VB, C_QI = 0, 2048, 4096, 6144, 6656, 7168
C_QC, C_KC, C_VC, C_RC = 8192, 9216, 10240, 12288
C_KIA, C_KIB, C_WI, C_GL = 14336, 14464, 14592, 14720
C_GZ = 15360
N_PROJ = C_GZ + N_BRANCH * D_MODEL

VMEM_LIMIT = 56 * 1024 * 1024


def _params(sem):
    return pltpu.CompilerParams(dimension_semantics=sem, vmem_limit_bytes=VMEM_LIMIT)


def _sigmoid(x):
    return 1.0 / (1.0 + jnp.exp(-x))


def _log_sigmoid(x):
    return jnp.minimum(x, 0.0) - jnp.log(1.0 + jnp.exp(-jnp.abs(x)))


def _dot(a, b):
    return jnp.dot(a, b, preferred_element_type=F32)


def _dot_nt(a, b):
    return lax.dot_general(a, b, (((1,), (1,)), ((), ())), preferred_element_type=F32)


def _dot_tn(a, b):
    return lax.dot_general(a, b, (((0,), (0,)), ((), ())), preferred_element_type=F32)


def _mm_body(x_ref, w_ref, o_ref, acc_ref):
    k = pl.program_id(2)

    @pl.when(k == 0)
    def _init():
        acc_ref[...] = jnp.zeros_like(acc_ref)

    acc_ref[...] += _dot(x_ref[...].astype(MXU_DTYPE), w_ref[...].astype(MXU_DTYPE))

    @pl.when(k == pl.num_programs(2) - 1)
    def _fin():
        o_ref[...] = acc_ref[...].astype(o_ref.dtype)


def _matmul(x, w, lead, out_dtype, tm, tn, tk):
    m_dim, k_dim = x.shape
    n_dim = w.shape[-1]
    nl = len(lead)
    return pl.pallas_call(
        _mm_body,
        grid=(m_dim // tm, n_dim // tn, k_dim // tk),
        in_specs=[pl.BlockSpec((tm, tk), lambda m, n, k: (m, k)),
                  pl.BlockSpec((None,) * nl + (tk, tn), lambda m, n, k: lead + (k, n))],
        out_specs=pl.BlockSpec((tm, tn), lambda m, n, k: (m, n)),
        out_shape=jax.ShapeDtypeStruct((m_dim, n_dim), out_dtype),
        scratch_shapes=[pltpu.VMEM((tm, tn), F32)],
        compiler_params=_params(("parallel", "parallel", "arbitrary")),
    )(x, w)


def _ln_body(x_ref, y_ref, g_ref, b_ref, o_ref, ob_ref):
    s = ALPHA * x_ref[...] + y_ref[...]
    mu = jnp.mean(s, axis=-1, keepdims=True)
    d = s - mu
    var = jnp.mean(d * d, axis=-1, keepdims=True)
    o = d * lax.rsqrt(var + LN_EPS) * g_ref[...] + b_ref[...]
    o_ref[...] = o
    ob_ref[...] = o.astype(BF16)


def _ln_residual(x, y, g, b, tm=256):
    t_dim = x.shape[0]
    row = pl.BlockSpec((tm, D_MODEL), lambda m: (m, 0))
    vec = pl.BlockSpec((1, D_MODEL), lambda m: (0, 0))
    return pl.pallas_call(
        _ln_body,
        grid=(t_dim // tm,),
        in_specs=[row, row, vec, vec],
        out_specs=[row, row],
        out_shape=[jax.ShapeDtypeStruct((t_dim, D_MODEL), F32), jax.ShapeDtypeStruct((t_dim, D_MODEL), BF16)],
        compiler_params=_params(("parallel",)),
    )(x, y, g.reshape(1, D_MODEL), b.reshape(1, D_MODEL))


def _glu_body(x_ref, wg_ref, wu_ref, o_ref, accg_ref, accu_ref):
    k = pl.program_id(2)

    @pl.when(k == 0)
    def _init():
        accg_ref[...] = jnp.zeros_like(accg_ref)
        accu_ref[...] = jnp.zeros_like(accu_ref)

    x = x_ref[...]
    accg_ref[...] += _dot(x, wg_ref[...].astype(MXU_DTYPE))
    accu_ref[...] += _dot(x, wu_ref[...].astype(MXU_DTYPE))

    @pl.when(k == pl.num_programs(2) - 1)
    def _fin():
        g = accg_ref[...]
        o_ref[...] = (g * _sigmoid(g) * accu_ref[...]).astype(o_ref.dtype)


def _glu_dense(xb, wg, wu, layer, tm=1024, tn=1024, tk=1024):
    t_dim, k_dim = xb.shape
    n_dim = wg.shape[-1]
    wspec = pl.BlockSpec((None, tk, tn), lambda m, n, k: (layer, k, n))
    return pl.pallas_call(
        _glu_body,
        grid=(t_dim // tm, n_dim // tn, k_dim // tk),
        in_specs=[pl.BlockSpec((tm, tk), lambda m, n, k: (m, k)), wspec, wspec],
        out_specs=pl.BlockSpec((tm, tn), lambda m, n, k: (m, n)),
        out_shape=jax.ShapeDtypeStruct((t_dim, n_dim), BF16),
        scratch_shapes=[pltpu.VMEM((tm, tn), F32), pltpu.VMEM((tm, tn), F32)],
        compiler_params=_params(("parallel", "parallel", "arbitrary")),
    )(xb, wg, wu)


MOE_TM = 512
MOE_G = 256
MOE_SRC = 512


def _moe_plan(combine, n_exp, tm, g, src):
    i32 = jnp.int32
    t_dim = combine.shape[0]
    n_rows = 2 * t_dim + n_exp * tm
    mask = combine[:, :n_exp] > 0.0
    mi = mask.astype(i32)
    cum = jnp.cumsum(mi, axis=0)
    cnt = cum[-1]
    gs = (cnt + tm - 1) // tm * tm
    g_end = jnp.cumsum(gs)
    g0 = g_end - gs
    rowid = jnp.where(mask, g0[None, :] + cum - mi, -1).astype(i32)
    used_rows = g_end[-1]

    n_mt = n_rows // tm
    mt_used = used_rows // tm
    mt_exp = jnp.minimum(jnp.searchsorted(g_end, jnp.arange(n_mt, dtype=i32) * tm, side="right"), n_exp - 1)

    n_gt = n_rows // g
    n_sb = t_dim // src
    gt_used = used_rows // g
    r0 = jnp.arange(n_gt, dtype=i32) * g
    gt_exp = jnp.minimum(jnp.searchsorted(g_end, r0, side="right"), n_exp - 1).astype(i32)
    rank0 = r0 - g0[gt_exp]
    rank1 = jnp.minimum(rank0 + g, cnt[gt_exp]) - 1
    cb = cum[src - 1::src].T
    cb_t = cb[gt_exp]
    fb = jnp.sum(cb_t <= rank0[:, None], axis=1)
    lb = jnp.sum(cb_t <= rank1[:, None], axis=1)
    has_rows = rank1 >= rank0
    fb = jnp.where(has_rows, jnp.minimum(fb, n_sb - 1), 0)
    lb = jnp.where(has_rows, jnp.minimum(lb, n_sb - 1), 0)
    nblk = jnp.where(jnp.arange(n_gt) < gt_used, lb - fb + 1, 0)
    off_end = jnp.cumsum(nblk)
    off = off_end - nblk
    n_items = n_gt + n_exp * n_sb
    w = jnp.arange(n_items, dtype=i32)
    it_valid = w < off_end[-1]
    it_tile = jnp.minimum(jnp.searchsorted(off_end, w, side="right"), jnp.maximum(gt_used - 1, 0)).astype(i32)
    it_blk = jnp.clip(fb[it_tile] + w - off[it_tile], 0, lb[it_tile]).astype(i32)
    it_first = (w == off[it_tile])
    gather = jnp.stack([it_tile, it_blk, gt_exp[it_tile], it_first.astype(i32), it_valid.astype(i32)])

    ce = cum[g - 1::g]
    cs = jnp.concatenate([jnp.zeros((1, n_exp), i32), ce[:-1]], axis=0)
    a = g0[None, :] + cs
    b = g0[None, :] + ce
    blk0 = jnp.clip(a // g, 0, n_gt - 1)
    blk1 = jnp.clip((b - 1) // g, 0, n_gt - 1)
    v0 = b > a
    v1 = v0 & (blk1 > blk0)
    c_blk = jnp.stack([blk0, jnp.where(v1, blk1, blk0)], axis=-1).reshape(-1).astype(i32)
    c_val = jnp.stack([v0, v1], axis=-1).reshape(-1).astype(i32)
    return dict(n_rows=n_rows, rowid=rowid, mt_used=mt_used.reshape(1).astype(i32), mt_exp=mt_exp.astype(i32),
                gather=gather, c_blk=c_blk, c_val=c_val)


def _moe_gather_body(it_ref, x_ref, rid_ref, o_ref, *, g):
    w = pl.program_id(0)
    tile, e, first, valid = it_ref[0, w], it_ref[2, w], it_ref[3, w], it_ref[4, w]

    @pl.when(valid == 1)
    def _():
        rid = rid_ref[pl.ds(e, 1), :]
        rows = tile * g + lax.broadcasted_iota(jnp.int32, (g, rid.shape[1]), 0)
        part = _dot((rid == rows).astype(MXU_DTYPE), x_ref[...]).astype(o_ref.dtype)

        @pl.when(first == 1)
        def _set():
            o_ref[...] = part

        @pl.when(first == 0)
        def _add():
            o_ref[...] += part


def _moe_gather(xb, rowid_t, plan, g, src):
    d = xb.shape[1]
    items = plan["gather"]
    return pl.pallas_call(
        functools.partial(_moe_gather_body, g=g),
        grid_spec=pltpu.PrefetchScalarGridSpec(
            num_scalar_prefetch=1, grid=(items.shape[1],),
            in_specs=[pl.BlockSpec((src, d), lambda w, it: (it[1, w], 0)),
                      pl.BlockSpec((rowid_t.shape[0], src), lambda w, it: (0, it[1, w]))],
            out_specs=pl.BlockSpec((g, d), lambda w, it: (it[0, w], 0))),
        out_shape=jax.ShapeDtypeStruct((plan["n_rows"], d), xb.dtype),
        compiler_params=_params(("arbitrary",)),
    )(items, xb, rowid_t)


def _moe_up_body(used_ref, exp_ref, x_ref, wg_ref, wu_ref, o_ref, accg_ref, accu_ref):
    i, k = pl.program_id(0), pl.program_id(2)

    @pl.when(i < used_ref[0])
    def _():
        @pl.when(k == 0)
        def _init():
            accg_ref[...] = jnp.zeros_like(accg_ref)
            accu_ref[...] = jnp.zeros_like(accu_ref)

        x = x_ref[...]
        accg_ref[...] += _dot(x, wg_ref[...].astype(MXU_DTYPE))
        accu_ref[...] += _dot(x, wu_ref[...].astype(MXU_DTYPE))

        @pl.when(k == pl.num_programs(2) - 1)
        def _fin():
            gate = accg_ref[...]
            o_ref[...] = (gate * _sigmoid(gate) * accu_ref[...]).astype(o_ref.dtype)


def _grouped_index(nn, nk):
    def pick(i, n, k, used):
        live = i < used[0]
        last = jnp.maximum(used[0] - 1, 0)
        return jnp.where(live, i, last), jnp.where(live, n, nn - 1), jnp.where(live, k, nk - 1)
    return pick


def _moe_up(xg, wg, wu, layer, plan, tm, tn=1024, tk=1024):
    n_rows, d = xg.shape
    f = wg.shape[-1]
    nn, nk = f // tn, d // tk
    pick = _grouped_index(nn, nk)

    def x_map(i, n, k, used, exp):
        ii, _, kk = pick(i, n, k, used)
        return ii, kk

    def w_map(i, n, k, used, exp):
        ii, n2, kk = pick(i, n, k, used)
        return layer, exp[ii], kk, n2

    def o_map(i, n, k, used, exp):
        ii, n2, _ = pick(i, n, k, used)
        return ii, n2

    wspec = pl.BlockSpec((None, None, tk, tn), w_map)
    return pl.pallas_call(
        _moe_up_body,
        grid_spec=pltpu.PrefetchScalarGridSpec(
            num_scalar_prefetch=2, grid=(n_rows // tm, nn, nk),
            in_specs=[pl.BlockSpec((tm, tk), x_map), wspec, wspec],
            out_specs=pl.BlockSpec((tm, tn), o_map),
            scratch_shapes=[pltpu.VMEM((tm, tn), F32), pltpu.VMEM((tm, tn), F32)]),
        out_shape=jax.ShapeDtypeStruct((n_rows, f), xg.dtype),
        compiler_params=_params(("arbitrary", "arbitrary", "arbitrary")),
    )(plan["mt_used"], plan["mt_exp"], xg, wg, wu)


def _moe_down_body(used_ref, exp_ref, x_ref, w_ref, hi_ref, lo_ref, acc_ref):
    i, k = pl.program_id(0), pl.program_id(2)

    @pl.when(i < used_ref[0])
    def _():
        @pl.when(k == 0)
        def _init():
            acc_ref[...] = jnp.zeros_like(acc_ref)

        acc_ref[...] += _dot(x_ref[...], w_ref[...].astype(MXU_DTYPE))

        @pl.when(k == pl.num_programs(2) - 1)
        def _fin():
            y = acc_ref[...]
            hi = y.astype(hi_ref.dtype)
            hi_ref[...] = hi
            lo_ref[...] = (y - hi.astype(F32)).astype(lo_ref.dtype)


def _moe_down(h, wd, layer, plan, tm, tn=1024, tk=1024):
    n_rows, f = h.shape
    d = wd.shape[-1]
    nn, nk = d // tn, f // tk
    pick = _grouped_index(nn, nk)

    def x_map(i, n, k, used, exp):
        ii, _, kk = pick(i, n, k, used)
        return ii, kk

    def w_map(i, n, k, used, exp):
        ii, n2, kk = pick(i, n, k, used)
        return layer, exp[ii], kk, n2

    def o_map(i, n, k, used, exp):
        ii, n2, _ = pick(i, n, k, used)
        return ii, n2

    piece = jax.ShapeDtypeStruct((n_rows, d), h.dtype)
    return pl.pallas_call(
        _moe_down_body,
        grid_spec=pltpu.PrefetchScalarGridSpec(
            num_scalar_prefetch=2, grid=(n_rows // tm, nn, nk),
            in_specs=[pl.BlockSpec((tm, tk), x_map), pl.BlockSpec((None, None, tk, tn), w_map)],
            out_specs=[pl.BlockSpec((tm, tn), o_map), pl.BlockSpec((tm, tn), o_map)],
            scratch_shapes=[pltpu.VMEM((tm, tn), F32)]),
        out_shape=[piece, piece],
        compiler_params=_params(("arbitrary", "arbitrary", "arbitrary")),
    )(plan["mt_used"], plan["mt_exp"], h, wd)


def _moe_combine_body(blk_ref, val_ref, hi_ref, lo_ref, rid_ref, c_ref, o_ref, *, g, n_exp):
    m, e, s = pl.program_id(0), pl.program_id(1), pl.program_id(2)
    item = (m * n_exp + e) * 2 + s

    @pl.when((e == 0) & (s == 0))
    def _init():
        o_ref[...] = jnp.zeros_like(o_ref)

    @pl.when(val_ref[item] == 1)
    def _():
        rid = rid_ref[...]
        rid_e = jnp.sum(jnp.where(lax.broadcasted_iota(jnp.int32, rid.shape, 1) == e, rid, 0), axis=1, keepdims=True)
        c = c_ref[...]
        c_e = jnp.sum(jnp.where(lax.broadcasted_iota(jnp.int32, c.shape, 1) == e, c, 0.0), axis=1, keepdims=True)
        rows = blk_ref[item] * g + lax.broadcasted_iota(jnp.int32, (g, g), 1)
        onehot = (rid_e == rows).astype(MXU_DTYPE)
        o_ref[...] += c_e * (_dot(onehot, hi_ref[...]) + _dot(onehot, lo_ref[...]))


def _moe_combine(y_hi, y_lo, rowid, combine, plan, g):
    t_dim, n_exp = rowid.shape
    d = y_hi.shape[1]

    def y_map(m, e, s, blk, val):
        return blk[(m * n_exp + e) * 2 + s], 0

    return pl.pallas_call(
        functools.partial(_moe_combine_body, g=g, n_exp=n_exp),
        grid_spec=pltpu.PrefetchScalarGridSpec(
            num_scalar_prefetch=2, grid=(t_dim // g, n_exp, 2),
            in_specs=[pl.BlockSpec((g, d), y_map), pl.BlockSpec((g, d), y_map),
                      pl.BlockSpec((g, n_exp), lambda m, e, s, blk, val: (m, 0)),
                      pl.BlockSpec((g, combine.shape[1]), lambda m, e, s, blk, val: (m, 0))],
            out_specs=pl.BlockSpec((g, d), lambda m, e, s, blk, val: (m, 0))),
        out_shape=jax.ShapeDtypeStruct((t_dim, d), F32),
        compiler_params=_params(("arbitrary", "arbitrary", "arbitrary")),
    )(plan["c_blk"], plan["c_val"], y_hi, y_lo, rowid, combine)


def _moe_routed(xb, combine, wg, wu, wd, layer, tm=MOE_TM, g=MOE_G, src=MOE_SRC, tn=1024, tk=1024):
    n_exp = wg.shape[1]
    plan = _moe_plan(combine, n_exp, tm, g, src)
    xg = _moe_gather(xb, plan["rowid"].T, plan, g, src)
    h = _moe_up(xg, wg, wu, layer, plan, tm, tn, tk)
    y_hi, y_lo = _moe_down(h, wd, layer, plan, tm, tn, tk)
    return _moe_combine(y_hi, y_lo, plan["rowid"], combine, plan, g)


def _router_body(x_ref, w_ref, c_ref):
    logits = jnp.dot(x_ref[...], w_ref[...], preferred_element_type=F32, precision=HIGHEST)
    lane = lax.broadcasted_iota(jnp.int32, logits.shape, 1)
    logits = jnp.where(lane < N_EXPERTS, logits, -jnp.inf)
    m1 = jnp.max(logits, axis=1, keepdims=True)
    i1 = jnp.min(jnp.where(logits == m1, lane, LANE), axis=1, keepdims=True)
    rest = jnp.where(lane == i1, -jnp.inf, logits)
    m2 = jnp.max(rest, axis=1, keepdims=True)
    i2 = jnp.min(jnp.where(rest == m2, lane, LANE), axis=1, keepdims=True)
    e2 = jnp.exp(m2 - m1)
    w1 = 1.0 / (1.0 + e2)
    w2 = e2 / (1.0 + e2)
    c_ref[...] = jnp.where(lane == i1, w1, 0.0) + jnp.where(lane == i2, w2, 0.0)


def _router(x, w_pad, tm=512):
    t_dim = x.shape[0]
    return pl.pallas_call(
        _router_body,
        grid=(t_dim // tm,),
        in_specs=[pl.BlockSpec((tm, D_MODEL), lambda m: (m, 0)),
                  pl.BlockSpec((D_MODEL, LANE), lambda m: (0, 0))],
        out_specs=pl.BlockSpec((tm, LANE), lambda m: (m, 0)),
        out_shape=jax.ShapeDtypeStruct((t_dim, LANE), F32),
        compiler_params=_params(("parallel",)),
    )(x, w_pad)


def _merge_body(br_ref, w_ref, gz_ref, bg_ref, o_ref, acc_ref):
    j = pl.program_id(2)

    @pl.when(j == 0)
    def _init():
        acc_ref[...] = jnp.zeros_like(acc_ref)

    pj = _dot(br_ref[...].astype(MXU_DTYPE), w_ref[...].astype(MXU_DTYPE))
    acc_ref[...] += _sigmoid(gz_ref[...] + bg_ref[...]) * pj

    @pl.when(j == N_BRANCH - 1)
    def _fin():
        o_ref[...] = acc_ref[...].astype(o_ref.dtype)


def _merge(branches, w_branch, proj, b_gate, layer, tm=1024, tn=1024):
    nt = D_MODEL // tn
    gz_blk = C_GZ // tn
    return pl.pallas_call(
        _merge_body,
        grid=(T_ALL // tm, nt, N_BRANCH),
        in_specs=[pl.BlockSpec((None, tm, W_BRANCH), lambda m, n, j: (j, m, 0)),
                  pl.BlockSpec((None, None, W_BRANCH, tn), lambda m, n, j: (layer, j, 0, n)),
                  pl.BlockSpec((tm, tn), lambda m, n, j: (m, gz_blk + j * nt + n)),
                  pl.BlockSpec((1, tn), lambda m, n, j: (0, j * nt + n))],
        out_specs=pl.BlockSpec((tm, tn), lambda m, n, j: (m, n)),
        out_shape=jax.ShapeDtypeStruct((T_ALL, D_MODEL), BF16),
        scratch_shapes=[pltpu.VMEM((tm, tn), F32)],
        compiler_params=_params(("parallel", "parallel", "arbitrary")),
    )(branches, w_branch, proj, b_gate.reshape(1, N_BRANCH * D_MODEL))


def _lru_body(xa_ref, ga_ref, buf_ref, h0_ref, cw_ref, cb_ref, wrg_ref, brg_ref, wig_ref, big_ref, lam_ref,
              o_ref, hlast_ref, xp_ref, a_ref, u_ref, h_ref, *, tl):
    t = pl.program_id(1)

    @pl.when(t == 0)
    def _init():
        xp_ref[5:8, :] = buf_ref[...]
        h_ref[...] = h0_ref[...]

    xp_ref[8:8 + tl, :] = xa_ref[...]
    xc = cb_ref[...] + xp_ref[8:8 + tl, :] * cw_ref[3:4, :]
    for j in range(CONV_W - 1):
        xc = xc + xp_ref[5 + j:5 + j + tl, :] * cw_ref[j:j + 1, :]
    xp_ref[5:8, :] = xa_ref[tl - 3:tl, :]

    xcb = xc.astype(MXU_DTYPE)
    for n in range(NB_A):
        sl = slice(n * BW_A, (n + 1) * BW_A)
        xs = xcb[:, sl]
        r = _sigmoid(_dot(xs, wrg_ref[n].astype(MXU_DTYPE)) + brg_ref[:, sl])
        i = _sigmoid(_dot(xs, wig_ref[n].astype(MXU_DTYPE)) + big_ref[:, sl])
        lam = lam_ref[:, sl]
        softplus = jnp.maximum(-lam, 0.0) + jnp.log(1.0 + jnp.exp(-jnp.abs(lam)))
        log_a = -LRU_C * r * softplus
        a_ref[:, sl] = jnp.exp(log_a)
        u_ref[:, sl] = jnp.sqrt(1.0 - jnp.exp(2.0 * log_a)) * (i * xc[:, sl])

    def step(s, h):
        h = a_ref[pl.ds(s, 1), :] * h + u_ref[pl.ds(s, 1), :]
        a_ref[pl.ds(s, 1), :] = h
        return h

    h = lax.fori_loop(0, tl, step, h_ref[...], unroll=8)
    h_ref[...] = h
    hlast_ref[...] = h
    g = ga_ref[...]
    gelu = 0.5 * g * (1.0 + jnp.tanh(0.7978845608028654 * (g + 0.044715 * (g * g * g))))
    o_ref[...] = (a_ref[...] * gelu).astype(o_ref.dtype)


def _branch_a(proj, row0, nb, seq, tl, conv_buf, h0, cw, cb, wrg, brg, wig, big, lam):
    nt = seq // tl
    rb0 = row0 // tl
    vec = pl.BlockSpec((1, W_A), lambda b, t: (0, 0))
    blk = pl.BlockSpec((NB_A, BW_A, BW_A), lambda b, t: (0, 0, 0))
    out, h_last = pl.pallas_call(
        functools.partial(_lru_body, tl=tl),
        grid=(nb, nt),
        in_specs=[pl.BlockSpec((tl, W_A), lambda b, t: (rb0 + b * nt + t, C_XA // W_A)),
                  pl.BlockSpec((tl, W_A), lambda b, t: (rb0 + b * nt + t, C_GA // W_A)),
                  pl.BlockSpec((None, CONV_W - 1, W_A), lambda b, t: (b, 0, 0)),
                  pl.BlockSpec((None, 1, W_A), lambda b, t: (b, 0, 0)),
                  pl.BlockSpec((CONV_W, W_A), lambda b, t: (0, 0)),
                  vec, blk, vec, blk, vec, vec],
        out_specs=[pl.BlockSpec((tl, W_A), lambda b, t: (b * nt + t, 0)),
                   pl.BlockSpec((None, 1, W_A), lambda b, t: (b, 0, 0))],
        out_shape=[jax.ShapeDtypeStruct((nb * seq, W_A), BF16), jax.ShapeDtypeStruct((nb, 1, W_A), F32)],
        scratch_shapes=[pltpu.VMEM((tl + 8, W_A), F32), pltpu.VMEM((tl, W_A), F32), pltpu.VMEM((tl, W_A), F32),
                        pltpu.VMEM((1, W_A), F32)],
        compiler_params=_params(("parallel", "arbitrary")),
    )(proj, proj, conv_buf, h0.reshape(nb, 1, W_A), cw, cb.reshape(1, W_A), wrg, brg.reshape(1, W_A),
      wig, big.reshape(1, W_A), lam.reshape(1, W_A))
    return out, h_last.reshape(nb, W_A)


def _gla_body(q_ref, k_ref, v_ref, rc_ref, gl_ref, wgg_ref, wggt_ref, bgr_ref, bgc_ref, g_ref, s0_ref,
              o_ref, s_ref, *, ck):
    c = pl.program_id(2)

    @pl.when(c == 0)
    def _init():
        s_ref[...] = s0_ref[...]

    gl = gl_ref[...].astype(MXU_DTYPE)
    lg = _log_sigmoid(_dot(gl, wgg_ref[...].astype(MXU_DTYPE)) + bgr_ref[...]) / GATE_NORM
    lg_t = _log_sigmoid(_dot_nt(wggt_ref[...].astype(MXU_DTYPE), gl) + bgc_ref[...]) / GATE_NORM
    row = lax.broadcasted_iota(jnp.int32, (ck, ck), 0)
    col = lax.broadcasted_iota(jnp.int32, (ck, ck), 1)
    tri = row >= col
    bcum = jnp.dot(tri.astype(F32), lg, preferred_element_type=F32, precision=HIGHEST)
    b_last = bcum[ck - 1:ck, :]
    b_last_col = jnp.sum(lg_t, axis=1, keepdims=True)

    k = k_ref[...]
    qe = (q_ref[...] * (DK_C ** -0.5) * jnp.exp(bcum)).astype(MXU_DTYPE)
    ke = (k * jnp.exp(-bcum)).astype(MXU_DTYPE)
    kt = (k * jnp.exp(b_last - bcum)).astype(MXU_DTYPE)
    vb = v_ref[...].astype(MXU_DTYPE)
    att = jnp.where(tri, _dot_nt(qe, ke), 0.0)
    s = s_ref[...]
    o = _dot(att.astype(MXU_DTYPE), vb) + _dot(qe, s.astype(MXU_DTYPE))
    s_ref[...] = jnp.exp(b_last_col) * s + _dot_tn(kt, vb)

    o = o * lax.rsqrt(jnp.mean(o * o, axis=-1, keepdims=True) + RMS_EPS) * g_ref[...]
    rc = rc_ref[...]
    o_ref[...] = (o * (rc * _sigmoid(rc))).astype(o_ref.dtype)


def _branch_c(proj, row0, nb, seq, ck, s0, wgg_pad, wggt_pad, bgg, gla_g):
    nc = seq // ck
    rb0 = row0 // ck

    def rows(cb):
        return lambda b, h, c: (rb0 + b * nc + c, cb)

    out, s_fin = pl.pallas_call(
        functools.partial(_gla_body, ck=ck),
        grid=(nb, H_C, nc),
        in_specs=[pl.BlockSpec((ck, DK_C), lambda b, h, c: (rb0 + b * nc + c, C_QC // DK_C + h)),
                  pl.BlockSpec((ck, DK_C), lambda b, h, c: (rb0 + b * nc + c, C_KC // DK_C + h)),
                  pl.BlockSpec((ck, DV_C), lambda b, h, c: (rb0 + b * nc + c, C_VC // DV_C + h)),
                  pl.BlockSpec((ck, DV_C), lambda b, h, c: (rb0 + b * nc + c, C_RC // DV_C + h)),
                  pl.BlockSpec((ck, LANE), lambda b, h, c: (rb0 + b * nc + c, C_GL // LANE)),
                  pl.BlockSpec((LANE, DK_C), lambda b, h, c: (0, h)),
                  pl.BlockSpec((DK_C, LANE), lambda b, h, c: (h, 0)),
                  pl.BlockSpec((1, DK_C), lambda b, h, c: (0, h)),
                  pl.BlockSpec((DK_C, 1), lambda b, h, c: (h, 0)),
                  pl.BlockSpec((1, DV_C), lambda b, h, c: (0, 0)),
                  pl.BlockSpec((None, None, DK_C, DV_C), lambda b, h, c: (b, h, 0, 0))],
        out_specs=[pl.BlockSpec((ck, DV_C), lambda b, h, c: (b * nc + c, h)),
                   pl.BlockSpec((None, None, DK_C, DV_C), lambda b, h, c: (b, h, 0, 0))],
        out_shape=[jax.ShapeDtypeStruct((nb * seq, H_C * DV_C), BF16),
                   jax.ShapeDtypeStruct((nb, H_C, DK_C, DV_C), F32)],
        compiler_params=_params(("parallel", "parallel", "arbitrary")),
    )(proj, proj, proj, proj, proj, wgg_pad, wggt_pad, bgg.reshape(1, H_C * DK_C), bgg.reshape(H_C * DK_C, 1),
      gla_g.reshape(1, DV_C), s0)
    return out, s_fin


def _dsa_body(*refs, tq, lc, lk, lpad, chunked):
    if lc:
        (q_ref, qi_ref, wi_ref, kn_ref, vn_ref, kan_ref, kbn_ref, kc_ref, vc_ref, kac_ref, kbc_ref,
         o_ref, k_s, v_s, ka_s, kb_s, key_s) = refs
    else:
        (q_ref, qi_ref, wi_ref, kn_ref, vn_ref, kan_ref, kbn_ref,
         o_ref, k_s, v_s, ka_s, kb_s, key_s) = refs
    qt = pl.program_id(1)
    n_keys = lc + lk

    @pl.when(qt == 0)
    def _stage_keys():
        if lc:
            k_s[0:lc, :] = kc_ref[...].astype(MXU_DTYPE)
            v_s[0:lc, :] = vc_ref[...].astype(MXU_DTYPE)
            ka_s[0:lc, :] = kac_ref[...].astype(MXU_DTYPE)
            kb_s[0:lc, :] = kbc_ref[...].astype(MXU_DTYPE)
        k_s[lc:n_keys, :] = kn_ref[...].astype(MXU_DTYPE)
        v_s[lc:n_keys, :] = vn_ref[...].astype(MXU_DTYPE)
        ka_s[lc:n_keys, :] = kan_ref[...].astype(MXU_DTYPE)
        kb_s[lc:n_keys, :] = kbn_ref[...].astype(MXU_DTYPE)
        if lpad > n_keys:
            k_s[n_keys:lpad, :] = jnp.zeros((lpad - n_keys, N_KV * HD_B), MXU_DTYPE)
            v_s[n_keys:lpad, :] = jnp.zeros((lpad - n_keys, N_KV * HD_B), MXU_DTYPE)
            ka_s[n_keys:lpad, :] = jnp.zeros((lpad - n_keys, LANE), MXU_DTYPE)
            kb_s[n_keys:lpad, :] = jnp.zeros((lpad - n_keys, LANE), MXU_DTYPE)

    qi = qi_ref[...].astype(MXU_DTYPE)
    wi = wi_ref[...] * IDX_W_SCALE
    ka = ka_s[...]
    kb = kb_s[...]
    score = jnp.zeros((tq, lpad), F32)
    for p in range(H_I // 2):
        qp = qi[:, p * LANE:(p + 1) * LANE]
        score = score + wi[:, 2 * p:2 * p + 1] * jnp.maximum(_dot_nt(qp, ka), 0.0)
        score = score + wi[:, 2 * p + 1:2 * p + 2] * jnp.maximum(_dot_nt(qp, kb), 0.0)

    score = jnp.where(score == 0.0, 0.0, score)
    bits = lax.bitcast_convert_type(score, jnp.int32)
    key = bits ^ ((bits >> 31) & 0x7FFFFFFF)
    col = lax.broadcasted_iota(jnp.int32, (tq, lpad), 1)
    if chunked:
        pos = qt * tq + lax.broadcasted_iota(jnp.int32, (tq, lpad), 0)
        valid = col < (pos // CHUNK + 1) * CHUNK
    else:
        valid = col < n_keys
    key_s[...] = jnp.where(valid, key, INT_MIN)

    def count_ge(cand):
        return jnp.sum((key_s[...] >= cand).astype(jnp.int32), axis=1, keepdims=True)

    prefix = jnp.where(count_ge(jnp.zeros((tq, 1), jnp.int32)) >= TOPK, 0, INT_MIN).astype(jnp.int32)

    def search(i, prefix):
        cand = prefix | jnp.left_shift(jnp.int32(1), 30 - i)
        return jnp.where(count_ge(cand) >= TOPK, cand, prefix)

    prefix = lax.fori_loop(0, 31, search, prefix)
    bias = jnp.where((key_s[...] >= prefix) & valid, 0.0, -jnp.inf)

    q = q_ref[...]
    n_rep = H_B // N_KV
    bias_g = jnp.concatenate([bias] * n_rep, axis=0)
    for n in range(N_KV):
        kn = k_s[:, n * HD_B:(n + 1) * HD_B]
        vn = v_s[:, n * HD_B:(n + 1) * HD_B]
        qg = jnp.concatenate([q[:, (n * n_rep + g) * HD_B:(n * n_rep + g + 1) * HD_B] for g in range(n_rep)], axis=0)
        s = _dot_nt(qg.astype(MXU_DTYPE), kn) * (HD_B ** -0.5) + bias_g
        m = jnp.max(s, axis=1, keepdims=True)
        p = jnp.exp(s - m)
        den = jnp.sum(p, axis=1, keepdims=True)
        o = _dot(p.astype(MXU_DTYPE), vn) / den
        for g in range(n_rep):
            sl = slice((n * n_rep + g) * HD_B, (n * n_rep + g + 1) * HD_B)
            o_ref[:, sl] = o[g * tq:(g + 1) * tq, :].astype(o_ref.dtype)


def _branch_b(proj, row0, nb, seq, tq, chunked, cache=None):
    nq = seq // tq
    rq0 = row0 // tq
    rk0 = row0 // seq
    lc = 0 if cache is None else cache[0].shape[1]
    lpad = -(-(lc + seq) // LANE) * LANE
    hk = N_KV * HD_B

    def qrows(cb):
        return lambda b, t: (rq0 + b * nq + t, cb)

    def krows(cb):
        return lambda b, t: (rk0 + b, cb)

    in_specs = [pl.BlockSpec((tq, H_B * HD_B), qrows(C_QB // (H_B * HD_B))),
                pl.BlockSpec((tq, H_I * D_I), qrows(C_QI // (H_I * D_I))),
                pl.BlockSpec((tq, LANE), qrows(C_WI // LANE)),
                pl.BlockSpec((seq, hk), krows(C_KB // hk)),
                pl.BlockSpec((seq, hk), krows(C_VB // hk)),
                pl.BlockSpec((seq, LANE), krows(C_KIA // LANE)),
                pl.BlockSpec((seq, LANE), krows(C_KIB // LANE))]
    args = [proj] * 7
    if cache is not None:
        in_specs += [pl.BlockSpec((None, lc, hk), lambda b, t: (b, 0, 0)),
                     pl.BlockSpec((None, lc, hk), lambda b, t: (b, 0, 0)),
                     pl.BlockSpec((None, lc, LANE), lambda b, t: (b, 0, 0)),
                     pl.BlockSpec((None, lc, LANE), lambda b, t: (b, 0, 0))]
        args += list(cache)
    return pl.pallas_call(
        functools.partial(_dsa_body, tq=tq, lc=lc, lk=seq, lpad=lpad, chunked=chunked),
        grid=(nb, nq),
        in_specs=in_specs,
        out_specs=pl.BlockSpec((tq, H_B * HD_B), lambda b, t: (b * nq + t, 0)),
        out_shape=jax.ShapeDtypeStruct((nb * seq, H_B * HD_B), BF16),
        scratch_shapes=[pltpu.VMEM((lpad, hk), MXU_DTYPE), pltpu.VMEM((lpad, hk), MXU_DTYPE),
                        pltpu.VMEM((lpad, LANE), MXU_DTYPE), pltpu.VMEM((lpad, LANE), MXU_DTYPE),
                        pltpu.VMEM((tq, lpad), jnp.int32)],
        compiler_params=_params(("parallel", "arbitrary")),
    )(*args)


def _relayout_w_in(w_in):
    def z(n):
        return jnp.zeros(w_in.shape[:2] + (n,), w_in.dtype)

    ki = w_in[..., 8192:8256]
    wi = w_in[..., 8256:8272]
    gl = w_in[..., 14416:14432]
    parts = [w_in[..., :8192], w_in[..., 8272:14416],
             ki, z(LANE - D_I), z(LANE - D_I), ki, wi, z(LANE - H_I), gl, z(LANE - GATE_RANK),
             z(C_GZ - C_GL - LANE), w_in[..., 14432:]]
    return jnp.concatenate(parts, axis=-1).astype(BF16)


def kernel(x_prompt, x_sample, cache_k, cache_v, cache_kidx, state_lru, state_conv, state_gla, w_in, conv_w, conv_b, w_rec_gate, b_rec_gate, w_in_gate, b_in_gate, lru_lambda, w_gla_gate, b_gla_gate, gla_norm_g, w_branch, b_branch_gate, w_out, ln1_g, ln1_b, ln2_g, ln2_b, w_ff_gate, w_ff_up, w_ff_down, w_router, w_exp_gate, w_exp_up, w_exp_down):
    hk = N_KV * HD_B
    x = jnp.concatenate([x_prompt.reshape(T_PROMPT, D_MODEL), x_sample.reshape(T_SAMPLE, D_MODEL)], axis=0)
    xb = x.astype(BF16)

    w_in_p = _relayout_w_in(w_in)
    ff_pad = D_FF_PAD - D_FF
    w_ffg = jnp.pad(w_ff_gate, ((0, 0), (0, 0), (0, ff_pad))).astype(BF16)
    w_ffu = jnp.pad(w_ff_up, ((0, 0), (0, 0), (0, ff_pad))).astype(BF16)
    w_ffd = jnp.pad(w_ff_down, ((0, 0), (0, ff_pad), (0, 0))).astype(BF16)
    w_router_p = jnp.pad(w_router, ((0, 0), (0, 0), (0, LANE - N_EXPERTS)))
    wgg_p = jnp.pad(w_gla_gate, ((0, 0), (0, LANE - GATE_RANK), (0, 0)))
    wggt_p = jnp.swapaxes(wgg_p, 1, 2)
    cache_k2 = cache_k.reshape(DEPTH, DEC_BATCH, PAST_LEN, hk)
    cache_v2 = cache_v.reshape(DEPTH, DEC_BATCH, PAST_LEN, hk)
    cache_kia = jnp.pad(cache_kidx, ((0, 0), (0, 0), (0, 0), (0, LANE - D_I)))
    cache_kib = jnp.pad(cache_kidx, ((0, 0), (0, 0), (0, 0), (LANE - D_I, 0)))
    zeros_conv = jnp.zeros((BATCH, CONV_W - 1, W_A), F32)
    zeros_lru = jnp.zeros((BATCH, W_A), F32)
    zeros_gla = jnp.zeros((BATCH, H_C, DK_C, DV_C), F32)

    outs_p = [[] for _ in range(6)]
    outs_s = [[] for _ in range(6)]
    for l in range(DEPTH):
        proj = _matmul(xb, w_in_p, (l,), F32, tm=1536, tn=1024, tk=1024)

        lru_args = (conv_w[l], conv_b[l], w_rec_gate[l], b_rec_gate[l], w_in_gate[l], b_in_gate[l], lru_lambda[l])
        a_p, lru_p = _branch_a(proj, 0, BATCH, SEQ, 256, zeros_conv, zeros_lru, *lru_args)
        a_s, lru_s = _branch_a(proj, T_PROMPT, DEC_BATCH, DEC_SEQ, DEC_SEQ, state_conv[l], state_lru[l], *lru_args)

        b_p = _branch_b(proj, 0, BATCH, SEQ, 128, True)
        b_s = _branch_b(proj, T_PROMPT, DEC_BATCH, DEC_SEQ, DEC_SEQ, False,
                        cache=(cache_k2[l], cache_v2[l], cache_kia[l], cache_kib[l]))

        gla_args = (wgg_p[l], wggt_p[l], b_gla_gate[l], gla_norm_g[l])
        c_p, gla_p = _branch_c(proj, 0, BATCH, SEQ, 64, zeros_gla, *gla_args)
        c_s, gla_s = _branch_c(proj, T_PROMPT, DEC_BATCH, DEC_SEQ, DEC_SEQ, state_gla[l], *gla_args)

        branches = jnp.stack([jnp.concatenate([a_p, a_s], axis=0), jnp.concatenate([b_p, b_s], axis=0),
                              jnp.concatenate([c_p, c_s], axis=0)])
        mixed = _merge(branches, w_branch, proj, b_branch_gate[l], l)
        y = _matmul(mixed, w_out, (l,), F32, tm=1536, tn=1024, tk=512)
        x, xb = _ln_residual(x, y, ln1_g[l], ln1_b[l])

        i = l // 2
        if l % 2 == 0:
            h = _glu_dense(xb, w_ffg, w_ffu, i)
            f = _matmul(h, w_ffd, (i,), F32, tm=1536, tn=1024, tk=1024)
        else:
            combine = _router(x, w_router_p[i])
            f = _moe_routed(xb, combine, w_exp_gate, w_exp_up, w_exp_down, i)
        x, xb = _ln_residual(x, f, ln2_g[l], ln2_b[l])

        pp, ps = proj[:T_PROMPT], proj[T_PROMPT:]
        for dst, rows, nb, seq, lru_h, gla_st in ((outs_p, pp, BATCH, SEQ, lru_p, gla_p),
                                                  (outs_s, ps, DEC_BATCH, DEC_SEQ, lru_s, gla_s)):
            dst[0].append(rows[:, C_KB:C_KB + hk].reshape(nb, seq, N_KV, HD_B))
            dst[1].append(rows[:, C_VB:C_VB + hk].reshape(nb, seq, N_KV, HD_B))
            dst[2].append(rows[:, C_KIA:C_KIA + D_I].reshape(nb, seq, D_I))
            dst[3].append(lru_h)
            dst[4].append(rows[:, C_XA:C_XA + W_A].reshape(nb, seq, W_A)[:, seq - (CONV_W - 1):])
            dst[5].append(gla_st)

    k_p, v_p, ki_p, lru_po, conv_p, gla_po = [jnp.stack(o) for o in outs_p]
    k_s, v_s, ki_s, lru_so, conv_s, gla_so = [jnp.stack(o) for o in outs_s]
    return (x[:T_PROMPT].reshape(BATCH, SEQ, D_MODEL), x[T_PROMPT:].reshape(DEC_BATCH, DEC_SEQ, D_MODEL),
            k_p, v_p, ki_p, lru_po, conv_p, gla_po, k_s, v_s, ki_s, lru_so, conv_s, gla_so)
```

```python
import functools

import jax
import jax.numpy as jnp
from jax import lax
from jax.experimental import pallas as pl
from jax.experimental.pallas import tpu as pltpu

F32 = jnp.float32
BF16 = jnp.bfloat16
MXU_DTYPE = BF16
HIGHEST = lax.Precision.HIGHEST

D_MODEL = 4096
BATCH, SEQ = 4, 2048
DEPTH = 4
DEC_BATCH, DEC_SEQ = 32, 32
PAST_LEN = 1024
T_PROMPT = BATCH * SEQ
T_SAMPLE = DEC_BATCH * DEC_SEQ
T_ALL = T_PROMPT + T_SAMPLE
CHUNK = 64
W_BRANCH = 2048
W_A = W_BRANCH
NB_A = 16
BW_A = W_A // NB_A
CONV_W = 4
LRU_C = 8.0
H_B, HD_B, N_KV = 16, 128, 4
H_I, D_I = 16, 64
TOPK = 256
IDX_W_SCALE = (H_I ** -0.5) * (D_I ** -0.5)
H_C, DK_C, DV_C = 4, 256, 512
GATE_RANK = 16
GATE_NORM = 16.0
N_BRANCH = 3
D_FF = 11008
D_FF_PAD = 11264
N_EXPERTS = 8
D_FF_E = 7168
ALPHA = (2.0 * DEPTH) ** 0.25
LN_EPS = 1e-5
RMS_EPS = 1e-6
LANE = 128
INT_MIN = -2 ** 31

C_XA, C_GA, C_QB, C_KB, C_VB, C_QI = 0, 2048, 4096, 6144, 6656, 7168
C_QC, C_KC, C_VC, C_RC = 8192, 9216, 10240, 12288
C_KIA, C_KIB, C_WI, C_GL = 14336, 14464, 14592, 14720
C_GZ = 15360
N_PROJ = C_GZ + N_BRANCH * D_MODEL

VMEM_LIMIT = 56 * 1024 * 1024


def _params(sem):
    return pltpu.CompilerParams(dimension_semantics=sem, vmem_limit_bytes=VMEM_LIMIT)


def _sigmoid(x):
    return 1.0 / (1.0 + jnp.exp(-x))


def _log_sigmoid(x):
    return jnp.minimum(x, 0.0) - jnp.log(1.0 + jnp.exp(-jnp.abs(x)))


def _dot(a, b):
    return jnp.dot(a, b, preferred_element_type=F32)


def _dot_nt(a, b):
    return lax.dot_general(a, b, (((1,), (1,)), ((), ())), preferred_element_type=F32)


def _dot_tn(a, b):
    return lax.dot_general(a, b, (((0,), (0,)), ((), ())), preferred_element_type=F32)


def _mm_body(x_ref, w_ref, o_ref, acc_ref):
    k = pl.program_id(2)

    @pl.when(k == 0)
    def _init():
        acc_ref[...] = jnp.zeros_like(acc_ref)

    acc_ref[...] += _dot(x_ref[...].astype(MXU_DTYPE), w_ref[...].astype(MXU_DTYPE))

    @pl.when(k == pl.num_programs(2) - 1)
    def _fin():
        o_ref[...] = acc_ref[...].astype(o_ref.dtype)


def _mm_full_k_body(x_ref, w_ref, o_ref):
    o_ref[...] = _dot(x_ref[...].astype(MXU_DTYPE), w_ref[...].astype(MXU_DTYPE)).astype(o_ref.dtype)


def _matmul(x, w, lead, out_dtype, tm, tn, tk):
    m_dim, k_dim = x.shape
    n_dim = w.shape[-1]
    nl = len(lead)
    full_k = tk == k_dim
    return pl.pallas_call(
        _mm_full_k_body if full_k else _mm_body,
        grid=(m_dim // tm, n_dim // tn, k_dim // tk),
        in_specs=[pl.BlockSpec((tm, tk), lambda m, n, k: (m, k)),
                  pl.BlockSpec((None,) * nl + (tk, tn), lambda m, n, k: lead + (k, n))],
        out_specs=pl.BlockSpec((tm, tn), lambda m, n, k: (m, n)),
        out_shape=jax.ShapeDtypeStruct((m_dim, n_dim), out_dtype),
        scratch_shapes=[] if full_k else [pltpu.VMEM((tm, tn), F32)],
        compiler_params=_params(("parallel", "parallel", "arbitrary")),
    )(x, w)


def _ln_body(x_ref, y_ref, g_ref, b_ref, o_ref, ob_ref):
    s = ALPHA * x_ref[...] + y_ref[...]
    mu = jnp.mean(s, axis=-1, keepdims=True)
    d = s - mu
    var = jnp.mean(d * d, axis=-1, keepdims=True)
    o = d * lax.rsqrt(var + LN_EPS) * g_ref[...] + b_ref[...]
    o_ref[...] = o
    ob_ref[...] = o.astype(BF16)


def _ln_residual(x, y, g, b, tm=256):
    t_dim = x.shape[0]
    row = pl.BlockSpec((tm, D_MODEL), lambda m: (m, 0))
    vec = pl.BlockSpec((1, D_MODEL), lambda m: (0, 0))
    return pl.pallas_call(
        _ln_body,
        grid=(t_dim // tm,),
        in_specs=[row, row, vec, vec],
        out_specs=[row, row],
        out_shape=[jax.ShapeDtypeStruct((t_dim, D_MODEL), F32), jax.ShapeDtypeStruct((t_dim, D_MODEL), BF16)],
        compiler_params=_params(("parallel",)),
    )(x, y, g.reshape(1, D_MODEL), b.reshape(1, D_MODEL))


def _glu_body(x_ref, wg_ref, wu_ref, o_ref):
    x = x_ref[...]
    g = _dot(x, wg_ref[...].astype(MXU_DTYPE))
    u = _dot(x, wu_ref[...].astype(MXU_DTYPE))
    o_ref[...] = (g * _sigmoid(g) * u).astype(o_ref.dtype)


def _glu_dense(xb, wg, wu, layer, tm=1024, tn=512):
    t_dim, k_dim = xb.shape
    n_dim = wg.shape[-1]
    wspec = pl.BlockSpec((None, k_dim, tn), lambda m, n: (layer, 0, n))
    return pl.pallas_call(
        _glu_body,
        grid=(t_dim // tm, n_dim // tn),
        in_specs=[pl.BlockSpec((tm, k_dim), lambda m, n: (m, 0)), wspec, wspec],
        out_specs=pl.BlockSpec((tm, tn), lambda m, n: (m, n)),
        out_shape=jax.ShapeDtypeStruct((t_dim, n_dim), BF16),
        compiler_params=_params(("parallel", "parallel")),
    )(xb, wg, wu)


MOE_TM = 1024
MOE_G = 256
MOE_SRC = 512


def _moe_plan(combine, n_exp, tm, g, src):
    i32 = jnp.int32
    t_dim = combine.shape[0]
    n_rows = 2 * t_dim + n_exp * tm
    mask = combine[:, :n_exp] > 0.0
    mi = mask.astype(i32)
    cum = jnp.cumsum(mi, axis=0)
    cnt = cum[-1]
    gs = (cnt + tm - 1) // tm * tm
    g_end = jnp.cumsum(gs)
    g0 = g_end - gs
    rowid = jnp.where(mask, g0[None, :] + cum - mi, -1).astype(i32)
    used_rows = g_end[-1]

    n_mt = n_rows // tm
    mt_used = used_rows // tm
    mt_exp = jnp.minimum(jnp.searchsorted(g_end, jnp.arange(n_mt, dtype=i32) * tm, side="right"), n_exp - 1)

    n_gt = n_rows // g
    n_sb = t_dim // src
    gt_used = used_rows // g
    r0 = jnp.arange(n_gt, dtype=i32) * g
    gt_exp = jnp.minimum(jnp.searchsorted(g_end, r0, side="right"), n_exp - 1).astype(i32)
    rank0 = r0 - g0[gt_exp]
    rank1 = jnp.minimum(rank0 + g, cnt[gt_exp]) - 1
    cb = cum[src - 1::src].T
    cb_t = cb[gt_exp]
    fb = jnp.sum(cb_t <= rank0[:, None], axis=1)
    lb = jnp.sum(cb_t <= rank1[:, None], axis=1)
    has_rows = rank1 >= rank0
    fb = jnp.where(has_rows, jnp.minimum(fb, n_sb - 1), 0)
    lb = jnp.where(has_rows, jnp.minimum(lb, n_sb - 1), 0)
    nblk = jnp.where(jnp.arange(n_gt) < gt_used, lb - fb + 1, 0)
    off_end = jnp.cumsum(nblk)
    off = off_end - nblk
    n_items = n_gt + n_exp * n_sb
    w = jnp.arange(n_items, dtype=i32)
    it_valid = w < off_end[-1]
    it_tile = jnp.minimum(jnp.searchsorted(off_end, w, side="right"), jnp.maximum(gt_used - 1, 0)).astype(i32)
    it_blk = jnp.clip(fb[it_tile] + w - off[it_tile], 0, lb[it_tile]).astype(i32)
    it_first = (w == off[it_tile])
    gather = jnp.stack([it_tile, it_blk, gt_exp[it_tile], it_first.astype(i32), it_valid.astype(i32)])

    ce = cum[g - 1::g]
    cs = jnp.concatenate([jnp.zeros((1, n_exp), i32), ce[:-1]], axis=0)
    a = g0[None, :] + cs
    b = g0[None, :] + ce
    blk0 = jnp.clip(a // g, 0, n_gt - 1)
    blk1 = jnp.clip((b - 1) // g, 0, n_gt - 1)
    v0 = b > a
    v1 = v0 & (blk1 > blk0)
    c_blk = jnp.stack([blk0, jnp.where(v1, blk1, blk0)], axis=-1).reshape(-1).astype(i32)
    c_val = jnp.stack([v0, v1], axis=-1).reshape(-1).astype(i32)
    return dict(n_rows=n_rows, rowid=rowid, mt_used=mt_used.reshape(1).astype(i32), mt_exp=mt_exp.astype(i32),
                gather=gather, c_blk=c_blk, c_val=c_val)


def _moe_gather_body(it_ref, x_ref, rid_ref, o_ref, *, g):
    w = pl.program_id(0)
    tile, e, first, valid = it_ref[0, w], it_ref[2, w], it_ref[3, w], it_ref[4, w]

    @pl.when(valid == 1)
    def _():
        rid = rid_ref[pl.ds(e, 1), :]
        rows = tile * g + lax.broadcasted_iota(jnp.int32, (g, rid.shape[1]), 0)
        part = _dot((rid == rows).astype(MXU_DTYPE), x_ref[...]).astype(o_ref.dtype)

        @pl.when(first == 1)
        def _set():
            o_ref[...] = part

        @pl.when(first == 0)
        def _add():
            o_ref[...] += part


def _moe_gather(xb, rowid_t, plan, g, src):
    d = xb.shape[1]
    items = plan["gather"]
    return pl.pallas_call(
        functools.partial(_moe_gather_body, g=g),
        grid_spec=pltpu.PrefetchScalarGridSpec(
            num_scalar_prefetch=1, grid=(items.shape[1],),
            in_specs=[pl.BlockSpec((src, d), lambda w, it: (it[1, w], 0)),
                      pl.BlockSpec((rowid_t.shape[0], src), lambda w, it: (0, it[1, w]))],
            out_specs=pl.BlockSpec((g, d), lambda w, it: (it[0, w], 0))),
        out_shape=jax.ShapeDtypeStruct((plan["n_rows"], d), xb.dtype),
        compiler_params=_params(("arbitrary",)),
    )(items, xb, rowid_t)


def _moe_up_body(used_ref, exp_ref, x_ref, wg_ref, wu_ref, o_ref):
    @pl.when(pl.program_id(0) < used_ref[0])
    def _():
        x = x_ref[...]
        gate = _dot(x, wg_ref[...].astype(MXU_DTYPE))
        up = _dot(x, wu_ref[...].astype(MXU_DTYPE))
        o_ref[...] = (gate * _sigmoid(gate) * up).astype(o_ref.dtype)


def _grouped_index(nn, nk):
    def pick(i, n, k, used):
        live = i < used[0]
        last = jnp.maximum(used[0] - 1, 0)
        return jnp.where(live, i, last), jnp.where(live, n, nn - 1), jnp.where(live, k, nk - 1)
    return pick


def _moe_up(xg, wg, wu, layer, plan, tm, tn):
    n_rows, d = xg.shape
    f = wg.shape[-1]
    nn = f // tn
    pick = _grouped_index(nn, 1)

    def x_map(i, n, used, exp):
        return pick(i, n, 0, used)[0], 0

    def w_map(i, n, used, exp):
        ii, n2, _ = pick(i, n, 0, used)
        return layer, exp[ii], 0, n2

    def o_map(i, n, used, exp):
        return pick(i, n, 0, used)[:2]

    wspec = pl.BlockSpec((None, None, d, tn), w_map)
    return pl.pallas_call(
        _moe_up_body,
        grid_spec=pltpu.PrefetchScalarGridSpec(
            num_scalar_prefetch=2, grid=(n_rows // tm, nn),
            in_specs=[pl.BlockSpec((tm, d), x_map), wspec, wspec],
            out_specs=pl.BlockSpec((tm, tn), o_map)),
        out_shape=jax.ShapeDtypeStruct((n_rows, f), xg.dtype),
        compiler_params=_params(("arbitrary", "arbitrary")),
    )(plan["mt_used"], plan["mt_exp"], xg, wg, wu)


def _moe_down_body(used_ref, exp_ref, x_ref, w_ref, hi_ref, lo_ref, acc_ref):
    i, k = pl.program_id(0), pl.program_id(2)

    @pl.when(i < used_ref[0])
    def _():
        @pl.when(k == 0)
        def _init():
            acc_ref[...] = jnp.zeros_like(acc_ref)

        acc_ref[...] += _dot(x_ref[...], w_ref[...].astype(MXU_DTYPE))

        @pl.when(k == pl.num_programs(2) - 1)
        def _fin():
            y = acc_ref[...]
            hi = y.astype(hi_ref.dtype)
            hi_ref[...] = hi
            lo_ref[...] = (y - hi.astype(F32)).astype(lo_ref.dtype)


def _moe_down(h, wd, layer, plan, tm, tn, tk):
    n_rows, f = h.shape
    d = wd.shape[-1]
    nn, nk = d // tn, f // tk
    pick = _grouped_index(nn, nk)

    def x_map(i, n, k, used, exp):
        ii, _, kk = pick(i, n, k, used)
        return ii, kk

    def w_map(i, n, k, used, exp):
        ii, n2, kk = pick(i, n, k, used)
        return layer, exp[ii], kk, n2

    def o_map(i, n, k, used, exp):
        ii, n2, _ = pick(i, n, k, used)
        return ii, n2

    piece = jax.ShapeDtypeStruct((n_rows, d), h.dtype)
    return pl.pallas_call(
        _moe_down_body,
        grid_spec=pltpu.PrefetchScalarGridSpec(
            num_scalar_prefetch=2, grid=(n_rows // tm, nn, nk),
            in_specs=[pl.BlockSpec((tm, tk), x_map), pl.BlockSpec((None, None, tk, tn), w_map)],
            out_specs=[pl.BlockSpec((tm, tn), o_map), pl.BlockSpec((tm, tn), o_map)],
            scratch_shapes=[pltpu.VMEM((tm, tn), F32)]),
        out_shape=[piece, piece],
        compiler_params=_params(("arbitrary", "arbitrary", "arbitrary")),
    )(plan["mt_used"], plan["mt_exp"], h, wd)


def _moe_combine_body(blk_ref, val_ref, hi_ref, lo_ref, rid_ref, c_ref, o_ref, *, g, n_exp):
    m, e, s = pl.program_id(0), pl.program_id(1), pl.program_id(2)
    item = (m * n_exp + e) * 2 + s

    @pl.when((e == 0) & (s == 0))
    def _init():
        o_ref[...] = jnp.zeros_like(o_ref)

    @pl.when(val_ref[item] == 1)
    def _():
        rid = rid_ref[...]
        rid_e = jnp.sum(jnp.where(lax.broadcasted_iota(jnp.int32, rid.shape, 1) == e, rid, 0), axis=1, keepdims=True)
        c = c_ref[...]
        c_e = jnp.sum(jnp.where(lax.broadcasted_iota(jnp.int32, c.shape, 1) == e, c, 0.0), axis=1, keepdims=True)
        rows = blk_ref[item] * g + lax.broadcasted_iota(jnp.int32, (g, g), 1)
        onehot = (rid_e == rows).astype(MXU_DTYPE)
        o_ref[...] += c_e * (_dot(onehot, hi_ref[...]) + _dot(onehot, lo_ref[...]))


def _moe_combine(y_hi, y_lo, rowid, combine, plan, g):
    t_dim, n_exp = rowid.shape
    d = y_hi.shape[1]

    def y_map(m, e, s, blk, val):
        return blk[(m * n_exp + e) * 2 + s], 0

    return pl.pallas_call(
        functools.partial(_moe_combine_body, g=g, n_exp=n_exp),
        grid_spec=pltpu.PrefetchScalarGridSpec(
            num_scalar_prefetch=2, grid=(t_dim // g, n_exp, 2),
            in_specs=[pl.BlockSpec((g, d), y_map), pl.BlockSpec((g, d), y_map),
                      pl.BlockSpec((g, n_exp), lambda m, e, s, blk, val: (m, 0)),
                      pl.BlockSpec((g, combine.shape[1]), lambda m, e, s, blk, val: (m, 0))],
            out_specs=pl.BlockSpec((g, d), lambda m, e, s, blk, val: (m, 0))),
        out_shape=jax.ShapeDtypeStruct((t_dim, d), F32),
        compiler_params=_params(("arbitrary", "arbitrary", "arbitrary")),
    )(plan["c_blk"], plan["c_val"], y_hi, y_lo, rowid, combine)


def _moe_routed(xb, combine, wg, wu, wd, layer, tm=MOE_TM, g=MOE_G, src=MOE_SRC, tn_up=256, tn_down=512, tk_down=3584):
    n_exp = wg.shape[1]
    plan = _moe_plan(combine, n_exp, tm, g, src)
    xg = _moe_gather(xb, plan["rowid"].T, plan, g, src)
    h = _moe_up(xg, wg, wu, layer, plan, tm, tn_up)
    y_hi, y_lo = _moe_down(h, wd, layer, plan, tm, tn_down, tk_down)
    return _moe_combine(y_hi, y_lo, plan["rowid"], combine, plan, g)


def _router_body(x_ref, w_ref, c_ref):
    logits = jnp.dot(x_ref[...], w_ref[...], preferred_element_type=F32, precision=HIGHEST)
    lane = lax.broadcasted_iota(jnp.int32, logits.shape, 1)
    logits = jnp.where(lane < N_EXPERTS, logits, -jnp.inf)
    m1 = jnp.max(logits, axis=1, keepdims=True)
    i1 = jnp.min(jnp.where(logits == m1, lane, LANE), axis=1, keepdims=True)
    rest = jnp.where(lane == i1, -jnp.inf, logits)
    m2 = jnp.max(rest, axis=1, keepdims=True)
    i2 = jnp.min(jnp.where(rest == m2, lane, LANE), axis=1, keepdims=True)
    e2 = jnp.exp(m2 - m1)
    w1 = 1.0 / (1.0 + e2)
    w2 = e2 / (1.0 + e2)
    c_ref[...] = jnp.where(lane == i1, w1, 0.0) + jnp.where(lane == i2, w2, 0.0)


def _router(x, w_pad, tm=512):
    t_dim = x.shape[0]
    return pl.pallas_call(
        _router_body,
        grid=(t_dim // tm,),
        in_specs=[pl.BlockSpec((tm, D_MODEL), lambda m: (m, 0)),
                  pl.BlockSpec((D_MODEL, LANE), lambda m: (0, 0))],
        out_specs=pl.BlockSpec((tm, LANE), lambda m: (m, 0)),
        out_shape=jax.ShapeDtypeStruct((t_dim, LANE), F32),
        compiler_params=_params(("parallel",)),
    )(x, w_pad)


def _merge_body(br_ref, w_ref, gz_ref, bg_ref, o_ref, acc_ref):
    j = pl.program_id(2)

    @pl.when(j == 0)
    def _init():
        acc_ref[...] = jnp.zeros_like(acc_ref)

    pj = _dot(br_ref[...].astype(MXU_DTYPE), w_ref[...].astype(MXU_DTYPE))
    acc_ref[...] += _sigmoid(gz_ref[...] + bg_ref[...]) * pj

    @pl.when(j == N_BRANCH - 1)
    def _fin():
        o_ref[...] = acc_ref[...].astype(o_ref.dtype)


def _merge(branches, w_branch, proj, b_gate, layer, tm=1024, tn=1024):
    nt = D_MODEL // tn
    gz_blk = C_GZ // tn
    return pl.pallas_call(
        _merge_body,
        grid=(T_ALL // tm, nt, N_BRANCH),
        in_specs=[pl.BlockSpec((None, tm, W_BRANCH), lambda m, n, j: (j, m, 0)),
                  pl.BlockSpec((None, None, W_BRANCH, tn), lambda m, n, j: (layer, j, 0, n)),
                  pl.BlockSpec((tm, tn), lambda m, n, j: (m, gz_blk + j * nt + n)),
                  pl.BlockSpec((1, tn), lambda m, n, j: (0, j * nt + n))],
        out_specs=pl.BlockSpec((tm, tn), lambda m, n, j: (m, n)),
        out_shape=jax.ShapeDtypeStruct((T_ALL, D_MODEL), BF16),
        scratch_shapes=[pltpu.VMEM((tm, tn), F32)],
        compiler_params=_params(("parallel", "parallel", "arbitrary")),
    )(branches, w_branch, proj, b_gate.reshape(1, N_BRANCH * D_MODEL))


def _lru_body(xa_ref, ga_ref, buf_ref, h0_ref, cw_ref, cb_ref, wrg_ref, brg_ref, wig_ref, big_ref, lam_ref,
              o_ref, hlast_ref, xp_ref, a_ref, u_ref, h_ref, *, tl):
    t = pl.program_id(1)

    @pl.when(t == 0)
    def _init():
        xp_ref[5:8, :] = buf_ref[...]
        h_ref[...] = h0_ref[...]

    xp_ref[8:8 + tl, :] = xa_ref[...]
    xc = cb_ref[...] + xp_ref[8:8 + tl, :] * cw_ref[3:4, :]
    for j in range(CONV_W - 1):
        xc = xc + xp_ref[5 + j:5 + j + tl, :] * cw_ref[j:j + 1, :]
    xp_ref[5:8, :] = xa_ref[tl - 3:tl, :]

    xcb = xc.astype(MXU_DTYPE)
    for n in range(NB_A):
        sl = slice(n * BW_A, (n + 1) * BW_A)
        xs = xcb[:, sl]
        r = _sigmoid(_dot(xs, wrg_ref[n].astype(MXU_DTYPE)) + brg_ref[:, sl])
        i = _sigmoid(_dot(xs, wig_ref[n].astype(MXU_DTYPE)) + big_ref[:, sl])
        lam = lam_ref[:, sl]
        softplus = jnp.maximum(-lam, 0.0) + jnp.log(1.0 + jnp.exp(-jnp.abs(lam)))
        log_a = -LRU_C * r * softplus
        a_ref[:, sl] = jnp.exp(log_a)
        u_ref[:, sl] = jnp.sqrt(1.0 - jnp.exp(2.0 * log_a)) * (i * xc[:, sl])

    def step(s, h):
        h = a_ref[pl.ds(s, 1), :] * h + u_ref[pl.ds(s, 1), :]
        a_ref[pl.ds(s, 1), :] = h
        return h

    h = lax.fori_loop(0, tl, step, h_ref[...], unroll=8)
    h_ref[...] = h
    hlast_ref[...] = h
    g = ga_ref[...]
    gelu = 0.5 * g * (1.0 + jnp.tanh(0.7978845608028654 * (g + 0.044715 * (g * g * g))))
    o_ref[...] = (a_ref[...] * gelu).astype(o_ref.dtype)


def _branch_a(proj, row0, nb, seq, tl, conv_buf, h0, cw, cb, wrg, brg, wig, big, lam):
    nt = seq // tl
    rb0 = row0 // tl
    vec = pl.BlockSpec((1, W_A), lambda b, t: (0, 0))
    blk = pl.BlockSpec((NB_A, BW_A, BW_A), lambda b, t: (0, 0, 0))
    out, h_last = pl.pallas_call(
        functools.partial(_lru_body, tl=tl),
        grid=(nb, nt),
        in_specs=[pl.BlockSpec((tl, W_A), lambda b, t: (rb0 + b * nt + t, C_XA // W_A)),
                  pl.BlockSpec((tl, W_A), lambda b, t: (rb0 + b * nt + t, C_GA // W_A)),
                  pl.BlockSpec((None, CONV_W - 1, W_A), lambda b, t: (b, 0, 0)),
                  pl.BlockSpec((None, 1, W_A), lambda b, t: (b, 0, 0)),
                  pl.BlockSpec((CONV_W, W_A), lambda b, t: (0, 0)),
                  vec, blk, vec, blk, vec, vec],
        out_specs=[pl.BlockSpec((tl, W_A), lambda b, t: (b * nt + t, 0)),
                   pl.BlockSpec((None, 1, W_A), lambda b, t: (b, 0, 0))],
        out_shape=[jax.ShapeDtypeStruct((nb * seq, W_A), BF16), jax.ShapeDtypeStruct((nb, 1, W_A), F32)],
        scratch_shapes=[pltpu.VMEM((tl + 8, W_A), F32), pltpu.VMEM((tl, W_A), F32), pltpu.VMEM((tl, W_A), F32),
                        pltpu.VMEM((1, W_A), F32)],
        compiler_params=_params(("parallel", "arbitrary")),
    )(proj, proj, conv_buf, h0.reshape(nb, 1, W_A), cw, cb.reshape(1, W_A), wrg, brg.reshape(1, W_A),
      wig, big.reshape(1, W_A), lam.reshape(1, W_A))
    return out, h_last.reshape(nb, W_A)


def _gla_body(q_ref, k_ref, v_ref, rc_ref, gl_ref, wgg_ref, wggt_ref, bgr_ref, bgc_ref, g_ref, s0_ref,
              o_ref, s_ref, *, ck):
    c = pl.program_id(1)

    @pl.when(c == 0)
    def _init():
        s_ref[...] = s0_ref[...]

    gl = gl_ref[...].astype(MXU_DTYPE)
    lg = _log_sigmoid(_dot(gl, wgg_ref[...].astype(MXU_DTYPE)) + bgr_ref[...]) / GATE_NORM
    lg_t = _log_sigmoid(_dot_nt(wggt_ref[...].astype(MXU_DTYPE), gl) + bgc_ref[...]) / GATE_NORM
    row = lax.broadcasted_iota(jnp.int32, (ck, ck), 0)
    col = lax.broadcasted_iota(jnp.int32, (ck, ck), 1)
    tri = row >= col
    bcum_all = jnp.dot(tri.astype(F32), lg, preferred_element_type=F32, precision=HIGHEST)
    b_last_col_all = jnp.sum(lg_t, axis=1, keepdims=True)

    for h in range(H_C):
        ks = slice(h * DK_C, (h + 1) * DK_C)
        vs = slice(h * DV_C, (h + 1) * DV_C)
        bcum = bcum_all[:, ks]
        b_last = bcum[ck - 1:ck, :]
        k = k_ref[:, ks]
        qe = (q_ref[:, ks] * (DK_C ** -0.5) * jnp.exp(bcum)).astype(MXU_DTYPE)
        ke = (k * jnp.exp(-bcum)).astype(MXU_DTYPE)
        kt = (k * jnp.exp(b_last - bcum)).astype(MXU_DTYPE)
        vb = v_ref[:, vs].astype(MXU_DTYPE)
        att = jnp.where(tri, _dot_nt(qe, ke), 0.0)
        s = s_ref[h]
        o = _dot(att.astype(MXU_DTYPE), vb) + _dot(qe, s.astype(MXU_DTYPE))
        s_ref[h] = jnp.exp(b_last_col_all[ks, :]) * s + _dot_tn(kt, vb)

        o = o * lax.rsqrt(jnp.mean(o * o, axis=-1, keepdims=True) + RMS_EPS) * g_ref[...]
        rc = rc_ref[:, vs]
        o_ref[:, vs] = (o * (rc * _sigmoid(rc))).astype(o_ref.dtype)


def _branch_c(proj, row0, nb, seq, ck, s0, wgg_pad, wggt_pad, bgg, gla_g):
    nc = seq // ck
    rb0 = row0 // ck
    hk, hv = H_C * DK_C, H_C * DV_C

    def rows(cb):
        return lambda b, c: (rb0 + b * nc + c, cb)

    def const(b, c):
        return (0, 0)

    state = pl.BlockSpec((None, H_C, DK_C, DV_C), lambda b, c: (b, 0, 0, 0))
    out, s_fin = pl.pallas_call(
        functools.partial(_gla_body, ck=ck),
        grid=(nb, nc),
        in_specs=[pl.BlockSpec((ck, hk), rows(C_QC // hk)),
                  pl.BlockSpec((ck, hk), rows(C_KC // hk)),
                  pl.BlockSpec((ck, hv), rows(C_VC // hv)),
                  pl.BlockSpec((ck, hv), rows(C_RC // hv)),
                  pl.BlockSpec((ck, LANE), rows(C_GL // LANE)),
                  pl.BlockSpec((LANE, hk), const),
                  pl.BlockSpec((hk, LANE), const),
                  pl.BlockSpec((1, hk), const),
                  pl.BlockSpec((hk, 1), const),
                  pl.BlockSpec((1, DV_C), const),
                  state],
        out_specs=[pl.BlockSpec((ck, hv), lambda b, c: (b * nc + c, 0)), state],
        out_shape=[jax.ShapeDtypeStruct((nb * seq, hv), BF16),
                   jax.ShapeDtypeStruct((nb, H_C, DK_C, DV_C), F32)],
        compiler_params=_params(("parallel", "arbitrary")),
    )(proj, proj, proj, proj, proj, wgg_pad, wggt_pad, bgg.reshape(1, hk), bgg.reshape(hk, 1),
      gla_g.reshape(1, DV_C), s0)
    return out, s_fin


def _dsa_body(*refs, tq, lc, lk, lpad, chunked, qt0):
    if lc:
        (q_ref, qi_ref, wi_ref, kn_ref, vn_ref, kan_ref, kbn_ref, kc_ref, vc_ref, kac_ref, kbc_ref,
         o_ref, k_s, v_s, ka_s, kb_s, key_s, u_s) = refs
    else:
        (q_ref, qi_ref, wi_ref, kn_ref, vn_ref, kan_ref, kbn_ref,
         o_ref, k_s, v_s, ka_s, kb_s, key_s, u_s) = refs
    qt = qt0 + pl.program_id(1)
    n_keys = lc + lk

    @pl.when(pl.program_id(1) == 0)
    def _stage_keys():
        if lc:
            k_s[0:lc, :] = kc_ref[...].astype(MXU_DTYPE)
            v_s[0:lc, :] = vc_ref[...].astype(MXU_DTYPE)
            ka_s[0:lc, :] = kac_ref[...].astype(MXU_DTYPE)
            kb_s[0:lc, :] = kbc_ref[...].astype(MXU_DTYPE)
        k_s[lc:n_keys, :] = kn_ref[0:lk, :].astype(MXU_DTYPE)
        v_s[lc:n_keys, :] = vn_ref[0:lk, :].astype(MXU_DTYPE)
        ka_s[lc:n_keys, :] = kan_ref[0:lk, :].astype(MXU_DTYPE)
        kb_s[lc:n_keys, :] = kbn_ref[0:lk, :].astype(MXU_DTYPE)
        if lpad > n_keys:
            k_s[n_keys:lpad, :] = jnp.zeros((lpad - n_keys, N_KV * HD_B), MXU_DTYPE)
            v_s[n_keys:lpad, :] = jnp.zeros((lpad - n_keys, N_KV * HD_B), MXU_DTYPE)
            ka_s[n_keys:lpad, :] = jnp.zeros((lpad - n_keys, LANE), MXU_DTYPE)
            kb_s[n_keys:lpad, :] = jnp.zeros((lpad - n_keys, LANE), MXU_DTYPE)
        for r0 in range(0, lpad, LANE):
            rr = r0 + lax.broadcasted_iota(jnp.int32, (LANE, lpad), 0)
            cc = lax.broadcasted_iota(jnp.int32, (LANE, lpad), 1)
            u_s[r0:r0 + LANE, :] = (rr < cc).astype(MXU_DTYPE)

    qi = qi_ref[...].astype(MXU_DTYPE)
    wi = wi_ref[...] * IDX_W_SCALE
    ka = ka_s[...]
    kb = kb_s[...]
    score = jnp.zeros((tq, lpad), F32)
    for p in range(H_I // 2):
        qp = qi[:, p * LANE:(p + 1) * LANE]
        score = score + wi[:, 2 * p:2 * p + 1] * jnp.maximum(_dot_nt(qp, ka), 0.0)
        score = score + wi[:, 2 * p + 1:2 * p + 2] * jnp.maximum(_dot_nt(qp, kb), 0.0)

    score = jnp.where(score == 0.0, 0.0, score)
    bits = lax.bitcast_convert_type(score, jnp.int32)
    key = bits ^ ((bits >> 31) & 0x7FFFFFFF)
    col = lax.broadcasted_iota(jnp.int32, (tq, lpad), 1)
    if chunked:
        pos = qt * tq + lax.broadcasted_iota(jnp.int32, (tq, lpad), 0)
        valid = col < (pos // CHUNK + 1) * CHUNK
    else:
        valid = col < n_keys
    key_s[...] = jnp.where(valid, key, INT_MIN)

    def count_ge(cand):
        return jnp.sum((key_s[...] >= cand).astype(jnp.int32), axis=1, keepdims=True)

    prefix = jnp.where(count_ge(jnp.zeros((tq, 1), jnp.int32)) >= TOPK, 0, INT_MIN).astype(jnp.int32)

    def search(i, prefix):
        cand = prefix | jnp.left_shift(jnp.int32(1), 30 - i)
        return jnp.where(count_ge(cand) >= TOPK, cand, prefix)

    prefix = lax.fori_loop(0, 31, search, prefix)

    keys = key_s[...]
    above = keys > prefix
    equal = (keys == prefix) & valid
    n_above = jnp.sum(above.astype(jnp.int32), axis=1, keepdims=True)
    rank = _dot(equal.astype(MXU_DTYPE), u_s[...])
    keep = above | (equal & (rank < (TOPK - n_above).astype(F32)))
    bias = jnp.where(keep, 0.0, -jnp.inf)

    q = q_ref[...]
    n_rep = H_B // N_KV
    bias_g = jnp.concatenate([bias] * n_rep, axis=0)
    for n in range(N_KV):
        kn = k_s[:, n * HD_B:(n + 1) * HD_B]
        vn = v_s[:, n * HD_B:(n + 1) * HD_B]
        qg = jnp.concatenate([q[:, (n * n_rep + g) * HD_B:(n * n_rep + g + 1) * HD_B] for g in range(n_rep)], axis=0)
        s = _dot_nt(qg.astype(MXU_DTYPE), kn) * (HD_B ** -0.5) + bias_g
        m = jnp.max(s, axis=1, keepdims=True)
        p = jnp.exp(s - m)
        den = jnp.sum(p, axis=1, keepdims=True)
        o = _dot(p.astype(MXU_DTYPE), vn) / den
        for g in range(n_rep):
            sl = slice((n * n_rep + g) * HD_B, (n * n_rep + g + 1) * HD_B)
            o_ref[:, sl] = o[g * tq:(g + 1) * tq, :].astype(o_ref.dtype)


def _branch_b(proj, row0, nb, seq, tq, chunked, cache=None, qt0=0, n_qt=None):
    nq = seq // tq
    n_qt = nq if n_qt is None else n_qt
    rq0 = row0 // tq
    rk0 = row0 // seq
    lc = 0 if cache is None else cache[0].shape[1]
    lk = (qt0 + n_qt) * tq if chunked else seq
    lpad = -(-(lc + lk) // LANE) * LANE
    hk = N_KV * HD_B

    def qrows(cb):
        return lambda b, t: (rq0 + b * nq + qt0 + t, cb)

    def krows(cb):
        return lambda b, t: (rk0 + b, cb)

    in_specs = [pl.BlockSpec((tq, H_B * HD_B), qrows(C_QB // (H_B * HD_B))),
                pl.BlockSpec((tq, H_I * D_I), qrows(C_QI // (H_I * D_I))),
                pl.BlockSpec((tq, LANE), qrows(C_WI // LANE)),
                pl.BlockSpec((seq, hk), krows(C_KB // hk)),
                pl.BlockSpec((seq, hk), krows(C_VB // hk)),
                pl.BlockSpec((seq, LANE), krows(C_KIA // LANE)),
                pl.BlockSpec((seq, LANE), krows(C_KIB // LANE))]
    args = [proj] * 7
    if cache is not None:
        in_specs += [pl.BlockSpec((None, lc, hk), lambda b, t: (b, 0, 0)),
                     pl.BlockSpec((None, lc, hk), lambda b, t: (b, 0, 0)),
                     pl.BlockSpec((None, lc, LANE), lambda b, t: (b, 0, 0)),
                     pl.BlockSpec((None, lc, LANE), lambda b, t: (b, 0, 0))]
        args += list(cache)
    return pl.pallas_call(
        functools.partial(_dsa_body, tq=tq, lc=lc, lk=lk, lpad=lpad, chunked=chunked, qt0=qt0),
        grid=(nb, n_qt),
        in_specs=in_specs,
        out_specs=pl.BlockSpec((tq, H_B * HD_B), lambda b, t: (b * n_qt + t, 0)),
        out_shape=jax.ShapeDtypeStruct((nb * n_qt * tq, H_B * HD_B), BF16),
        scratch_shapes=[pltpu.VMEM((lpad, hk), MXU_DTYPE), pltpu.VMEM((lpad, hk), MXU_DTYPE),
                        pltpu.VMEM((lpad, LANE), MXU_DTYPE), pltpu.VMEM((lpad, LANE), MXU_DTYPE),
                        pltpu.VMEM((tq, lpad), jnp.int32), pltpu.VMEM((lpad, lpad), MXU_DTYPE)],
        compiler_params=_params(("parallel", "arbitrary")),
    )(*args)


def _branch_b_prompt(proj, tq=128, tiles_per_call=2):
    nq = SEQ // tq
    bands = [_branch_b(proj, 0, BATCH, SEQ, tq, True, qt0=q0, n_qt=tiles_per_call).reshape(BATCH, -1, H_B * HD_B)
             for q0 in range(0, nq, tiles_per_call)]
    return jnp.concatenate(bands, axis=1).reshape(T_PROMPT, H_B * HD_B)


def _relayout_w_in(w_in):
    def z(n):
        return jnp.zeros(w_in.shape[:2] + (n,), w_in.dtype)

    ki = w_in[..., 8192:8256]
    wi = w_in[..., 8256:8272]
    gl = w_in[..., 14416:14432]
    parts = [w_in[..., :8192], w_in[..., 8272:14416],
             ki, z(LANE - D_I), z(LANE - D_I), ki, wi, z(LANE - H_I), gl, z(LANE - GATE_RANK),
             z(C_GZ - C_GL - LANE), w_in[..., 14432:]]
    return jnp.concatenate(parts, axis=-1).astype(BF16)


def kernel(x_prompt, x_sample, cache_k, cache_v, cache_kidx, state_lru, state_conv, state_gla, w_in, conv_w, conv_b, w_rec_gate, b_rec_gate, w_in_gate, b_in_gate, lru_lambda, w_gla_gate, b_gla_gate, gla_norm_g, w_branch, b_branch_gate, w_out, ln1_g, ln1_b, ln2_g, ln2_b, w_ff_gate, w_ff_up, w_ff_down, w_router, w_exp_gate, w_exp_up, w_exp_down):
    hk = N_KV * HD_B
    x = jnp.concatenate([x_prompt.reshape(T_PROMPT, D_MODEL), x_sample.reshape(T_SAMPLE, D_MODEL)], axis=0)
    xb = x.astype(BF16)

    w_in_p = _relayout_w_in(w_in)
    ff_pad = D_FF_PAD - D_FF
    w_ffg = jnp.pad(w_ff_gate, ((0, 0), (0, 0), (0, ff_pad))).astype(BF16)
    w_ffu = jnp.pad(w_ff_up, ((0, 0), (0, 0), (0, ff_pad))).astype(BF16)
    w_ffd = jnp.pad(w_ff_down, ((0, 0), (0, ff_pad), (0, 0))).astype(BF16)
    w_router_p = jnp.pad(w_router, ((0, 0), (0, 0), (0, LANE - N_EXPERTS)))
    wgg_p = jnp.pad(w_gla_gate, ((0, 0), (0, LANE - GATE_RANK), (0, 0)))
    wggt_p = jnp.swapaxes(wgg_p, 1, 2)
    cache_k2 = cache_k.reshape(DEPTH, DEC_BATCH, PAST_LEN, hk)
    cache_v2 = cache_v.reshape(DEPTH, DEC_BATCH, PAST_LEN, hk)
    cache_kia = jnp.pad(cache_kidx, ((0, 0), (0, 0), (0, 0), (0, LANE - D_I)))
    cache_kib = jnp.pad(cache_kidx, ((0, 0), (0, 0), (0, 0), (LANE - D_I, 0)))
    zeros_conv = jnp.zeros((BATCH, CONV_W - 1, W_A), F32)
    zeros_lru = jnp.zeros((BATCH, W_A), F32)
    zeros_gla = jnp.zeros((BATCH, H_C, DK_C, DV_C), F32)

    outs_p = [[] for _ in range(6)]
    outs_s = [[] for _ in range(6)]
    for l in range(DEPTH):
        proj = _matmul(xb, w_in_p, (l,), F32, tm=1024, tn=1024, tk=D_MODEL)

        lru_args = (conv_w[l], conv_b[l], w_rec_gate[l], b_rec_gate[l], w_in_gate[l], b_in_gate[l], lru_lambda[l])
        a_p, lru_p = _branch_a(proj, 0, BATCH, SEQ, 256, zeros_conv, zeros_lru, *lru_args)
        a_s, lru_s = _branch_a(proj, T_PROMPT, DEC_BATCH, DEC_SEQ, DEC_SEQ, state_conv[l], state_lru[l], *lru_args)

        b_p = _branch_b_prompt(proj)
        b_s = _branch_b(proj, T_PROMPT, DEC_BATCH, DEC_SEQ, DEC_SEQ, False,
                        cache=(cache_k2[l], cache_v2[l], cache_kia[l], cache_kib[l]))

        gla_args = (wgg_p[l], wggt_p[l], b_gla_gate[l], gla_norm_g[l])
        c_p, gla_p = _branch_c(proj, 0, BATCH, SEQ, 64, zeros_gla, *gla_args)
        c_s, gla_s = _branch_c(proj, T_PROMPT, DEC_BATCH, DEC_SEQ, DEC_SEQ, state_gla[l], *gla_args)

        branches = jnp.stack([jnp.concatenate([a_p, a_s], axis=0), jnp.concatenate([b_p, b_s], axis=0),
                              jnp.concatenate([c_p, c_s], axis=0)])
        mixed = _merge(branches, w_branch, proj, b_branch_gate[l], l)
        y = _matmul(mixed, w_out, (l,), F32, tm=1024, tn=512, tk=D_MODEL)
        x, xb = _ln_residual(x, y, ln1_g[l], ln1_b[l])

        i = l // 2
        if l % 2 == 0:
            h = _glu_dense(xb, w_ffg, w_ffu, i)
            f = _matmul(h, w_ffd, (i,), F32, tm=1024, tn=1024, tk=D_FF_PAD // 4)
        else:
            combine = _router(x, w_router_p[i])
            f = _moe_routed(xb, combine, w_exp_gate, w_exp_up, w_exp_down, i)
        x, xb = _ln_residual(x, f, ln2_g[l], ln2_b[l])

        pp, ps = proj[:T_PROMPT], proj[T_PROMPT:]
        for dst, rows, nb, seq, lru_h, gla_st in ((outs_p, pp, BATCH, SEQ, lru_p, gla_p),
                                                  (outs_s, ps, DEC_BATCH, DEC_SEQ, lru_s, gla_s)):
            dst[0].append(rows[:, C_KB:C_KB + hk].reshape(nb, seq, N_KV, HD_B))
            dst[1].append(rows[:, C_VB:C_VB + hk].reshape(nb, seq, N_KV, HD_B))
            dst[2].append(rows[:, C_KIA:C_KIA + D_I].reshape(nb, seq, D_I))
            dst[3].append(lru_h)
            dst[4].append(rows[:, C_XA:C_XA + W_A].reshape(nb, seq, W_A)[:, seq - (CONV_W - 1):])
            dst[5].append(gla_st)

    k_p, v_p, ki_p, lru_po, conv_p, gla_po = [jnp.stack(o) for o in outs_p]
    k_s, v_s, ki_s, lru_so, conv_s, gla_so = [jnp.stack(o) for o in outs_s]
    return (x[:T_PROMPT].reshape(BATCH, SEQ, D_MODEL), x[T_PROMPT:].reshape(DEC_BATCH, DEC_SEQ, D_MODEL),
            k_p, v_p, ki_p, lru_po, conv_p, gla_po, k_s, v_s, ki_s, lru_so, conv_s, gla_so)
```

```python
import functools

import jax
import jax.numpy as jnp
from jax import lax
from jax.experimental import pallas as pl
from jax.experimental.pallas import tpu as pltpu

F32 = jnp.float32
BF16 = jnp.bfloat16
MXU_DTYPE = BF16
HIGHEST = lax.Precision.HIGHEST

D_MODEL = 4096
BATCH, SEQ = 4, 2048
DEPTH = 4
DEC_BATCH, DEC_SEQ = 32, 32
PAST_LEN = 1024
T_PROMPT = BATCH * SEQ
T_SAMPLE = DEC_BATCH * DEC_SEQ
T_ALL = T_PROMPT + T_SAMPLE
CHUNK = 64
W_BRANCH = 2048
W_A = W_BRANCH
NB_A = 16
BW_A = W_A // NB_A
CONV_W = 4
LRU_C = 8.0
H_B, HD_B, N_KV = 16, 128, 4
H_I, D_I = 16, 64
TOPK = 256
IDX_W_SCALE = (H_I ** -0.5) * (D_I ** -0.5)
H_C, DK_C, DV_C = 4, 256, 512
GATE_RANK = 16
GATE_NORM = 16.0
N_BRANCH = 3
D_FF = 11008
D_FF_PAD = 11264
N_EXPERTS = 8
D_FF_E = 7168
ALPHA = (2.0 * DEPTH) ** 0.25
LN_EPS = 1e-5
RMS_EPS = 1e-6
LANE = 128
INT_MIN = -2 ** 31

C_XA, C_GA, C_QB, C_KB, C_VB, C_QI = 0, 2048, 4096, 6144, 6656, 7168
C_QC, C_KC, C_VC, C_RC = 8192, 9216, 10240, 12288
C_KIA, C_KIB, C_WI, C_GL = 14336, 14464, 14592, 14720
C_GZ = 15360
N_PROJ = C_GZ + N_BRANCH * D_MODEL

VMEM_LIMIT = 56 * 1024 * 1024


def _params(sem):
    return pltpu.CompilerParams(dimension_semantics=sem, vmem_limit_bytes=VMEM_LIMIT)


def _sigmoid(x):
    return 1.0 / (1.0 + jnp.exp(-x))


def _log_sigmoid(x):
    return jnp.minimum(x, 0.0) - jnp.log(1.0 + jnp.exp(-jnp.abs(x)))


def _dot(a, b):
    return jnp.dot(a, b, preferred_element_type=F32)


def _dot_nt(a, b):
    return lax.dot_general(a, b, (((1,), (1,)), ((), ())), preferred_element_type=F32)


def _dot_tn(a, b):
    return lax.dot_general(a, b, (((0,), (0,)), ((), ())), preferred_element_type=F32)


def _mm_body(x_ref, w_ref, o_ref, acc_ref):
    k = pl.program_id(2)

    @pl.when(k == 0)
    def _init():
        acc_ref[...] = jnp.zeros_like(acc_ref)

    acc_ref[...] += _dot(x_ref[...].astype(MXU_DTYPE), w_ref[...].astype(MXU_DTYPE))

    @pl.when(k == pl.num_programs(2) - 1)
    def _fin():
        o_ref[...] = acc_ref[...].astype(o_ref.dtype)


def _mm_full_k_body(x_ref, w_ref, o_ref):
    o_ref[...] = _dot(x_ref[...].astype(MXU_DTYPE), w_ref[...].astype(MXU_DTYPE)).astype(o_ref.dtype)


def _matmul(x, w, lead, out_dtype, tm, tn, tk):
    m_dim, k_dim = x.shape
    n_dim = w.shape[-1]
    nl = len(lead)
    full_k = tk == k_dim
    return pl.pallas_call(
        _mm_full_k_body if full_k else _mm_body,
        grid=(m_dim // tm, n_dim // tn, k_dim // tk),
        in_specs=[pl.BlockSpec((tm, tk), lambda m, n, k: (m, k)),
                  pl.BlockSpec((None,) * nl + (tk, tn), lambda m, n, k: lead + (k, n))],
        out_specs=pl.BlockSpec((tm, tn), lambda m, n, k: (m, n)),
        out_shape=jax.ShapeDtypeStruct((m_dim, n_dim), out_dtype),
        scratch_shapes=[] if full_k else [pltpu.VMEM((tm, tn), F32)],
        compiler_params=_params(("parallel", "parallel", "arbitrary")),
    )(x, w)


def _cast_pad_body(x_ref, o_ref, *, rows, cols):
    tr, tc = o_ref.shape
    r = pl.program_id(1) * tr + lax.broadcasted_iota(jnp.int32, (tr, tc), 0)
    c = pl.program_id(2) * tc + lax.broadcasted_iota(jnp.int32, (tr, tc), 1)
    o_ref[...] = jnp.where((r < rows) & (c < cols), x_ref[...], 0.0).astype(o_ref.dtype)


def _cast_pad(w, rows_pad, cols_pad, tr=512, tc=1024):
    n_l, rows, cols = w.shape
    assert (rows_pad - rows) < tr and (cols_pad - cols) < tc
    spec = pl.BlockSpec((None, tr, tc), lambda l, i, j: (l, i, j))
    return pl.pallas_call(
        functools.partial(_cast_pad_body, rows=rows, cols=cols),
        grid=(n_l, rows_pad // tr, cols_pad // tc),
        in_specs=[spec], out_specs=spec,
        out_shape=jax.ShapeDtypeStruct((n_l, rows_pad, cols_pad), BF16),
        compiler_params=_params(("parallel", "parallel", "parallel")),
    )(w)


def _ln_body(x_ref, y_ref, g_ref, b_ref, o_ref, ob_ref):
    s = ALPHA * x_ref[...] + y_ref[...]
    mu = jnp.mean(s, axis=-1, keepdims=True)
    d = s - mu
    var = jnp.mean(d * d, axis=-1, keepdims=True)
    o = d * lax.rsqrt(var + LN_EPS) * g_ref[...] + b_ref[...]
    o_ref[...] = o
    ob_ref[...] = o.astype(BF16)


def _ln_residual(x, y, g, b, tm=256):
    t_dim = x.shape[0]
    row = pl.BlockSpec((tm, D_MODEL), lambda m: (m, 0))
    vec = pl.BlockSpec((1, D_MODEL), lambda m: (0, 0))
    return pl.pallas_call(
        _ln_body,
        grid=(t_dim // tm,),
        in_specs=[row, row, vec, vec],
        out_specs=[row, row],
        out_shape=[jax.ShapeDtypeStruct((t_dim, D_MODEL), F32), jax.ShapeDtypeStruct((t_dim, D_MODEL), BF16)],
        compiler_params=_params(("parallel",)),
    )(x, y, g.reshape(1, D_MODEL), b.reshape(1, D_MODEL))


def _glu_body(x_ref, wg_ref, wu_ref, o_ref):
    x = x_ref[...]
    g = _dot(x, wg_ref[...].astype(MXU_DTYPE))
    u = _dot(x, wu_ref[...].astype(MXU_DTYPE))
    o_ref[...] = (g * _sigmoid(g) * u).astype(o_ref.dtype)


def _glu_dense(xb, wg, wu, layer, tm=1024, tn=512):
    t_dim, k_dim = xb.shape
    n_dim = wg.shape[-1]
    wspec = pl.BlockSpec((None, k_dim, tn), lambda m, n: (layer, 0, n))
    return pl.pallas_call(
        _glu_body,
        grid=(t_dim // tm, n_dim // tn),
        in_specs=[pl.BlockSpec((tm, k_dim), lambda m, n: (m, 0)), wspec, wspec],
        out_specs=pl.BlockSpec((tm, tn), lambda m, n: (m, n)),
        out_shape=jax.ShapeDtypeStruct((t_dim, n_dim), BF16),
        compiler_params=_params(("parallel", "parallel")),
    )(xb, wg, wu)


MOE_TM = 1024
MOE_G = 256
MOE_SRC = 512


def _moe_plan(combine, n_exp, tm, g, src):
    i32 = jnp.int32
    t_dim = combine.shape[0]
    n_rows = 2 * t_dim + n_exp * tm
    mask = combine[:, :n_exp] > 0.0
    mi = mask.astype(i32)
    cum = jnp.cumsum(mi, axis=0)
    cnt = cum[-1]
    gs = (cnt + tm - 1) // tm * tm
    g_end = jnp.cumsum(gs)
    g0 = g_end - gs
    rowid = jnp.where(mask, g0[None, :] + cum - mi, -1).astype(i32)
    used_rows = g_end[-1]

    n_mt = n_rows // tm
    mt_used = used_rows // tm
    mt_row0 = jnp.arange(n_mt, dtype=i32) * tm
    mt_exp = jnp.minimum(jnp.searchsorted(g_end, mt_row0, side="right"), n_exp - 1)
    mt_live = jnp.clip(cnt[mt_exp] - (mt_row0 - g0[mt_exp]), 0, tm)

    n_gt = n_rows // g
    n_sb = t_dim // src
    gt_used = used_rows // g
    r0 = jnp.arange(n_gt, dtype=i32) * g
    gt_exp = jnp.minimum(jnp.searchsorted(g_end, r0, side="right"), n_exp - 1).astype(i32)
    rank0 = r0 - g0[gt_exp]
    rank1 = jnp.minimum(rank0 + g, cnt[gt_exp]) - 1
    cb = cum[src - 1::src].T
    cb_t = cb[gt_exp]
    fb = jnp.sum(cb_t <= rank0[:, None], axis=1)
    lb = jnp.sum(cb_t <= rank1[:, None], axis=1)
    has_rows = rank1 >= rank0
    fb = jnp.where(has_rows, jnp.minimum(fb, n_sb - 1), 0)
    lb = jnp.where(has_rows, jnp.minimum(lb, n_sb - 1), 0)
    nblk = jnp.where(jnp.arange(n_gt) < gt_used, lb - fb + 1, 0)
    off_end = jnp.cumsum(nblk)
    off = off_end - nblk
    n_items = n_gt + n_exp * n_sb
    w = jnp.arange(n_items, dtype=i32)
    it_valid = w < off_end[-1]
    it_tile = jnp.minimum(jnp.searchsorted(off_end, w, side="right"), jnp.maximum(gt_used - 1, 0)).astype(i32)
    it_blk = jnp.clip(fb[it_tile] + w - off[it_tile], 0, lb[it_tile]).astype(i32)
    it_first = (w == off[it_tile])
    gather = jnp.stack([it_tile, it_blk, gt_exp[it_tile], it_first.astype(i32), it_valid.astype(i32)])

    ce = cum[g - 1::g]
    cs = jnp.concatenate([jnp.zeros((1, n_exp), i32), ce[:-1]], axis=0)
    a = g0[None, :] + cs
    b = g0[None, :] + ce
    blk0 = jnp.clip(a // g, 0, n_gt - 1)
    blk1 = jnp.clip((b - 1) // g, 0, n_gt - 1)
    v0 = b > a
    v1 = v0 & (blk1 > blk0)
    c_blk = jnp.stack([blk0, jnp.where(v1, blk1, blk0)], axis=-1).reshape(-1).astype(i32)
    c_val = jnp.stack([v0, v1], axis=-1).reshape(-1).astype(i32)
    return dict(n_rows=n_rows, rowid=rowid, mt_used=mt_used.reshape(1).astype(i32), mt_exp=mt_exp.astype(i32),
                mt_live=mt_live.astype(i32), gather=gather, c_blk=c_blk, c_val=c_val)


def _moe_gather_body(it_ref, x_ref, rid_ref, o_ref, *, g):
    w = pl.program_id(0)
    tile, e, first, valid = it_ref[0, w], it_ref[2, w], it_ref[3, w], it_ref[4, w]

    @pl.when(valid == 1)
    def _():
        rid = rid_ref[pl.ds(e, 1), :]
        rows = tile * g + lax.broadcasted_iota(jnp.int32, (g, rid.shape[1]), 0)
        part = _dot((rid == rows).astype(MXU_DTYPE), x_ref[...]).astype(o_ref.dtype)

        @pl.when(first == 1)
        def _set():
            o_ref[...] = part

        @pl.when(first == 0)
        def _add():
            o_ref[...] += part


def _moe_gather(xb, rowid_t, plan, g, src):
    d = xb.shape[1]
    items = plan["gather"]
    return pl.pallas_call(
        functools.partial(_moe_gather_body, g=g),
        grid_spec=pltpu.PrefetchScalarGridSpec(
            num_scalar_prefetch=1, grid=(items.shape[1],),
            in_specs=[pl.BlockSpec((src, d), lambda w, it: (it[1, w], 0)),
                      pl.BlockSpec((rowid_t.shape[0], src), lambda w, it: (0, it[1, w]))],
            out_specs=pl.BlockSpec((g, d), lambda w, it: (it[0, w], 0))),
        out_shape=jax.ShapeDtypeStruct((plan["n_rows"], d), xb.dtype),
        compiler_params=_params(("arbitrary",)),
    )(items, xb, rowid_t)


def _moe_up_body(used_ref, exp_ref, live_ref, x_ref, wg_ref, wu_ref, o_ref):
    i = pl.program_id(0)
    half = x_ref.shape[0] // 2

    def run(rows):
        x = x_ref[0:rows, :]
        gate = _dot(x, wg_ref[...].astype(MXU_DTYPE))
        up = _dot(x, wu_ref[...].astype(MXU_DTYPE))
        o_ref[0:rows, :] = (gate * _sigmoid(gate) * up).astype(o_ref.dtype)

    @pl.when((i < used_ref[0]) & (live_ref[i] > half))
    def _full():
        run(2 * half)

    @pl.when((i < used_ref[0]) & (live_ref[i] <= half))
    def _half():
        run(half)


def _grouped_index(nn, nk):
    def pick(i, n, k, used):
        live = i < used[0]
        last = jnp.maximum(used[0] - 1, 0)
        return jnp.where(live, i, last), jnp.where(live, n, nn - 1), jnp.where(live, k, nk - 1)
    return pick


def _moe_up(xg, wg, wu, layer, plan, tm, tn):
    n_rows, d = xg.shape
    f = wg.shape[-1]
    nn = f // tn
    pick = _grouped_index(nn, 1)

    def x_map(i, n, used, exp, live):
        return pick(i, n, 0, used)[0], 0

    def w_map(i, n, used, exp, live):
        ii, n2, _ = pick(i, n, 0, used)
        return layer, exp[ii], 0, n2

    def o_map(i, n, used, exp, live):
        return pick(i, n, 0, used)[:2]

    wspec = pl.BlockSpec((None, None, d, tn), w_map)
    return pl.pallas_call(
        _moe_up_body,
        grid_spec=pltpu.PrefetchScalarGridSpec(
            num_scalar_prefetch=3, grid=(n_rows // tm, nn),
            in_specs=[pl.BlockSpec((tm, d), x_map), wspec, wspec],
            out_specs=pl.BlockSpec((tm, tn), o_map)),
        out_shape=jax.ShapeDtypeStruct((n_rows, f), xg.dtype),
        compiler_params=_params(("arbitrary", "arbitrary")),
    )(plan["mt_used"], plan["mt_exp"], plan["mt_live"], xg, wg, wu)


def _moe_down_body(used_ref, exp_ref, live_ref, x_ref, w_ref, hi_ref, lo_ref):
    i = pl.program_id(0)
    half = x_ref.shape[0] // 2

    def run(rows):
        y = _dot(x_ref[0:rows, :], w_ref[...].astype(MXU_DTYPE))
        hi = y.astype(hi_ref.dtype)
        hi_ref[0:rows, :] = hi
        lo_ref[0:rows, :] = (y - hi.astype(F32)).astype(lo_ref.dtype)

    @pl.when((i < used_ref[0]) & (live_ref[i] > half))
    def _full():
        run(2 * half)

    @pl.when((i < used_ref[0]) & (live_ref[i] <= half))
    def _half():
        run(half)


def _moe_down(h, wd, layer, plan, tm, tn):
    n_rows, f = h.shape
    d = wd.shape[-1]
    nn = d // tn
    pick = _grouped_index(nn, 1)

    def x_map(i, n, used, exp, live):
        return pick(i, n, 0, used)[0], 0

    def w_map(i, n, used, exp, live):
        ii, n2, _ = pick(i, n, 0, used)
        return layer, exp[ii], 0, n2

    def o_map(i, n, used, exp, live):
        return pick(i, n, 0, used)[:2]

    piece = jax.ShapeDtypeStruct((n_rows, d), h.dtype)
    return pl.pallas_call(
        _moe_down_body,
        grid_spec=pltpu.PrefetchScalarGridSpec(
            num_scalar_prefetch=3, grid=(n_rows // tm, nn),
            in_specs=[pl.BlockSpec((tm, f), x_map), pl.BlockSpec((None, None, f, tn), w_map)],
            out_specs=[pl.BlockSpec((tm, tn), o_map), pl.BlockSpec((tm, tn), o_map)]),
        out_shape=[piece, piece],
        compiler_params=_params(("arbitrary", "arbitrary")),
    )(plan["mt_used"], plan["mt_exp"], plan["mt_live"], h, wd)


def _moe_combine_body(blk_ref, val_ref, hi_ref, lo_ref, rid_ref, c_ref, o_ref, *, g, n_exp):
    m, e, s = pl.program_id(0), pl.program_id(1), pl.program_id(2)
    item = (m * n_exp + e) * 2 + s

    @pl.when((e == 0) & (s == 0))
    def _init():
        o_ref[...] = jnp.zeros_like(o_ref)

    @pl.when(val_ref[item] == 1)
    def _():
        rid = rid_ref[...]
        rid_e = jnp.sum(jnp.where(lax.broadcasted_iota(jnp.int32, rid.shape, 1) == e, rid, 0), axis=1, keepdims=True)
        c = c_ref[...]
        c_e = jnp.sum(jnp.where(lax.broadcasted_iota(jnp.int32, c.shape, 1) == e, c, 0.0), axis=1, keepdims=True)
        rows = blk_ref[item] * g + lax.broadcasted_iota(jnp.int32, (g, g), 1)
        onehot = (rid_e == rows).astype(MXU_DTYPE)
        o_ref[...] += c_e * (_dot(onehot, hi_ref[...]) + _dot(onehot, lo_ref[...]))


def _moe_combine(y_hi, y_lo, rowid, combine, plan, g):
    t_dim, n_exp = rowid.shape
    d = y_hi.shape[1]

    def y_map(m, e, s, blk, val):
        return blk[(m * n_exp + e) * 2 + s], 0

    return pl.pallas_call(
        functools.partial(_moe_combine_body, g=g, n_exp=n_exp),
        grid_spec=pltpu.PrefetchScalarGridSpec(
            num_scalar_prefetch=2, grid=(t_dim // g, n_exp, 2),
            in_specs=[pl.BlockSpec((g, d), y_map), pl.BlockSpec((g, d), y_map),
                      pl.BlockSpec((g, n_exp), lambda m, e, s, blk, val: (m, 0)),
                      pl.BlockSpec((g, combine.shape[1]), lambda m, e, s, blk, val: (m, 0))],
            out_specs=pl.BlockSpec((g, d), lambda m, e, s, blk, val: (m, 0))),
        out_shape=jax.ShapeDtypeStruct((t_dim, d), F32),
        compiler_params=_params(("arbitrary", "arbitrary", "arbitrary")),
    )(plan["c_blk"], plan["c_val"], y_hi, y_lo, rowid, combine)


def _moe_routed(xb, combine, wg, wu, wd, layer, tm=MOE_TM, g=MOE_G, src=MOE_SRC, tn_up=256, tn_down=256):
    n_exp = wg.shape[1]
    plan = _moe_plan(combine, n_exp, tm, g, src)
    xg = _moe_gather(xb, plan["rowid"].T, plan, g, src)
    h = _moe_up(xg, wg, wu, layer, plan, tm, tn_up)
    y_hi, y_lo = _moe_down(h, wd, layer, plan, tm, tn_down)
    return _moe_combine(y_hi, y_lo, plan["rowid"], combine, plan, g)


def _router_body(x_ref, w_ref, c_ref):
    logits = jnp.dot(x_ref[...], w_ref[...], preferred_element_type=F32, precision=HIGHEST)
    lane = lax.broadcasted_iota(jnp.int32, logits.shape, 1)
    logits = jnp.where(lane < N_EXPERTS, logits, -jnp.inf)
    m1 = jnp.max(logits, axis=1, keepdims=True)
    i1 = jnp.min(jnp.where(logits == m1, lane, LANE), axis=1, keepdims=True)
    rest = jnp.where(lane == i1, -jnp.inf, logits)
    m2 = jnp.max(rest, axis=1, keepdims=True)
    i2 = jnp.min(jnp.where(rest == m2, lane, LANE), axis=1, keepdims=True)
    e2 = jnp.exp(m2 - m1)
    w1 = 1.0 / (1.0 + e2)
    w2 = e2 / (1.0 + e2)
    c_ref[...] = jnp.where(lane == i1, w1, 0.0) + jnp.where(lane == i2, w2, 0.0)


def _router(x, w_pad, tm=512):
    t_dim = x.shape[0]
    return pl.pallas_call(
        _router_body,
        grid=(t_dim // tm,),
        in_specs=[pl.BlockSpec((tm, D_MODEL), lambda m: (m, 0)),
                  pl.BlockSpec((D_MODEL, LANE), lambda m: (0, 0))],
        out_specs=pl.BlockSpec((tm, LANE), lambda m: (m, 0)),
        out_shape=jax.ShapeDtypeStruct((t_dim, LANE), F32),
        compiler_params=_params(("parallel",)),
    )(x, w_pad)


def _merge_body(br_ref, w_ref, gz_ref, bg_ref, o_ref, acc_ref):
    j = pl.program_id(2)

    @pl.when(j == 0)
    def _init():
        acc_ref[...] = jnp.zeros_like(acc_ref)

    pj = _dot(br_ref[...].astype(MXU_DTYPE), w_ref[...].astype(MXU_DTYPE))
    acc_ref[...] += _sigmoid(gz_ref[...] + bg_ref[...]) * pj

    @pl.when(j == N_BRANCH - 1)
    def _fin():
        o_ref[...] = acc_ref[...].astype(o_ref.dtype)


def _merge(branches, w_branch, proj, b_gate, layer, tm=1024, tn=1024):
    nt = D_MODEL // tn
    gz_blk = C_GZ // tn
    return pl.pallas_call(
        _merge_body,
        grid=(T_ALL // tm, nt, N_BRANCH),
        in_specs=[pl.BlockSpec((None, tm, W_BRANCH), lambda m, n, j: (j, m, 0)),
                  pl.BlockSpec((None, None, W_BRANCH, tn), lambda m, n, j: (layer, j, 0, n)),
                  pl.BlockSpec((tm, tn), lambda m, n, j: (m, gz_blk + j * nt + n)),
                  pl.BlockSpec((1, tn), lambda m, n, j: (0, j * nt + n))],
        out_specs=pl.BlockSpec((tm, tn), lambda m, n, j: (m, n)),
        out_shape=jax.ShapeDtypeStruct((T_ALL, D_MODEL), BF16),
        scratch_shapes=[pltpu.VMEM((tm, tn), F32)],
        compiler_params=_params(("parallel", "parallel", "arbitrary")),
    )(branches, w_branch, proj, b_gate.reshape(1, N_BRANCH * D_MODEL))


def _lru_body(xa_ref, ga_ref, buf_ref, h0_ref, cw_ref, cb_ref, wrg_ref, brg_ref, wig_ref, big_ref, lam_ref,
              o_ref, hlast_ref, xp_ref, a_ref, u_ref, h_ref, *, tl):
    t = pl.program_id(1)

    @pl.when(t == 0)
    def _init():
        xp_ref[5:8, :] = buf_ref[...]
        h_ref[...] = h0_ref[...]

    xp_ref[8:8 + tl, :] = xa_ref[...]
    xc = cb_ref[...] + xp_ref[8:8 + tl, :] * cw_ref[3:4, :]
    for j in range(CONV_W - 1):
        xc = xc + xp_ref[5 + j:5 + j + tl, :] * cw_ref[j:j + 1, :]
    xp_ref[5:8, :] = xa_ref[tl - 3:tl, :]

    xcb = xc.astype(MXU_DTYPE)
    for n in range(NB_A):
        sl = slice(n * BW_A, (n + 1) * BW_A)
        xs = xcb[:, sl]
        r = _sigmoid(_dot(xs, wrg_ref[n].astype(MXU_DTYPE)) + brg_ref[:, sl])
        i = _sigmoid(_dot(xs, wig_ref[n].astype(MXU_DTYPE)) + big_ref[:, sl])
        lam = lam_ref[:, sl]
        softplus = jnp.maximum(-lam, 0.0) + jnp.log(1.0 + jnp.exp(-jnp.abs(lam)))
        log_a = -LRU_C * r * softplus
        a_ref[:, sl] = jnp.exp(log_a)
        u_ref[:, sl] = jnp.sqrt(1.0 - jnp.exp(2.0 * log_a)) * (i * xc[:, sl])

    def step(s, h):
        h = a_ref[pl.ds(s, 1), :] * h + u_ref[pl.ds(s, 1), :]
        a_ref[pl.ds(s, 1), :] = h
        return h

    h = lax.fori_loop(0, tl, step, h_ref[...], unroll=8)
    h_ref[...] = h
    hlast_ref[...] = h
    g = ga_ref[...]
    gelu = 0.5 * g * (1.0 + jnp.tanh(0.7978845608028654 * (g + 0.044715 * (g * g * g))))
    o_ref[...] = (a_ref[...] * gelu).astype(o_ref.dtype)


def _branch_a(proj, row0, nb, seq, tl, conv_buf, h0, cw, cb, wrg, brg, wig, big, lam):
    nt = seq // tl
    rb0 = row0 // tl
    vec = pl.BlockSpec((1, W_A), lambda b, t: (0, 0))
    blk = pl.BlockSpec((NB_A, BW_A, BW_A), lambda b, t: (0, 0, 0))
    out, h_last = pl.pallas_call(
        functools.partial(_lru_body, tl=tl),
        grid=(nb, nt),
        in_specs=[pl.BlockSpec((tl, W_A), lambda b, t: (rb0 + b * nt + t, C_XA // W_A)),
                  pl.BlockSpec((tl, W_A), lambda b, t: (rb0 + b * nt + t, C_GA // W_A)),
                  pl.BlockSpec((None, CONV_W - 1, W_A), lambda b, t: (b, 0, 0)),
                  pl.BlockSpec((None, 1, W_A), lambda b, t: (b, 0, 0)),
                  pl.BlockSpec((CONV_W, W_A), lambda b, t: (0, 0)),
                  vec, blk, vec, blk, vec, vec],
        out_specs=[pl.BlockSpec((tl, W_A), lambda b, t: (b * nt + t, 0)),
                   pl.BlockSpec((None, 1, W_A), lambda b, t: (b, 0, 0))],
        out_shape=[jax.ShapeDtypeStruct((nb * seq, W_A), BF16), jax.ShapeDtypeStruct((nb, 1, W_A), F32)],
        scratch_shapes=[pltpu.VMEM((tl + 8, W_A), F32), pltpu.VMEM((tl, W_A), F32), pltpu.VMEM((tl, W_A), F32),
                        pltpu.VMEM((1, W_A), F32)],
        compiler_params=_params(("parallel", "arbitrary")),
    )(proj, proj, conv_buf, h0.reshape(nb, 1, W_A), cw, cb.reshape(1, W_A), wrg, brg.reshape(1, W_A),
      wig, big.reshape(1, W_A), lam.reshape(1, W_A))
    return out, h_last.reshape(nb, W_A)


def _gla_body(*refs, ck, has_s0):
    if has_s0:
        q_ref, k_ref, v_ref, rc_ref, gl_ref, wgg_ref, wggt_ref, bgr_ref, bgc_ref, g_ref, s0_ref, o_ref, s_ref = refs
    else:
        q_ref, k_ref, v_ref, rc_ref, gl_ref, wgg_ref, wggt_ref, bgr_ref, bgc_ref, g_ref, o_ref, s_ref = refs
    c = pl.program_id(1)

    @pl.when(c == 0)
    def _init():
        s_ref[...] = s0_ref[...] if has_s0 else jnp.zeros_like(s_ref)

    gl = gl_ref[...].astype(MXU_DTYPE)
    lg = _log_sigmoid(_dot(gl, wgg_ref[...].astype(MXU_DTYPE)) + bgr_ref[...]) / GATE_NORM
    lg_t = _log_sigmoid(_dot_nt(wggt_ref[...].astype(MXU_DTYPE), gl) + bgc_ref[...]) / GATE_NORM
    row = lax.broadcasted_iota(jnp.int32, (ck, ck), 0)
    col = lax.broadcasted_iota(jnp.int32, (ck, ck), 1)
    tri = row >= col
    bcum_all = jnp.dot(tri.astype(F32), lg, preferred_element_type=F32, precision=HIGHEST)
    b_last_col_all = jnp.sum(lg_t, axis=1, keepdims=True)

    for h in range(H_C):
        ks = slice(h * DK_C, (h + 1) * DK_C)
        vs = slice(h * DV_C, (h + 1) * DV_C)
        bcum = bcum_all[:, ks]
        b_last = bcum[ck - 1:ck, :]
        k = k_ref[:, ks]
        qe = (q_ref[:, ks] * (DK_C ** -0.5) * jnp.exp(bcum)).astype(MXU_DTYPE)
        ke = (k * jnp.exp(-bcum)).astype(MXU_DTYPE)
        kt = (k * jnp.exp(b_last - bcum)).astype(MXU_DTYPE)
        vb = v_ref[:, vs].astype(MXU_DTYPE)
        att = jnp.where(tri, _dot_nt(qe, ke), 0.0)
        s = s_ref[h]
        o = _dot(att.astype(MXU_DTYPE), vb) + _dot(qe, s.astype(MXU_DTYPE))
        s_ref[h] = jnp.exp(b_last_col_all[ks, :]) * s + _dot_tn(kt, vb)

        o = o * lax.rsqrt(jnp.mean(o * o, axis=-1, keepdims=True) + RMS_EPS) * g_ref[...]
        rc = rc_ref[:, vs]
        o_ref[:, vs] = (o * (rc * _sigmoid(rc))).astype(o_ref.dtype)


def _branch_c(proj, row0, nb, seq, ck, s0, layer, wgg_pad, wggt_pad, bgg, gla_g):
    nc = seq // ck
    rb0 = row0 // ck
    hk, hv = H_C * DK_C, H_C * DV_C

    def rows(cb):
        return lambda b, c: (rb0 + b * nc + c, cb)

    def const(b, c):
        return (0, 0)

    state = pl.BlockSpec((None, H_C, DK_C, DV_C), lambda b, c: (b, 0, 0, 0))
    has_s0 = s0 is not None
    s0_spec = [pl.BlockSpec((None, None, H_C, DK_C, DV_C), lambda b, c: (layer, b, 0, 0, 0))] if has_s0 else []
    s0_arg = [s0] if has_s0 else []
    out, s_fin = pl.pallas_call(
        functools.partial(_gla_body, ck=ck, has_s0=has_s0),
        grid=(nb, nc),
        in_specs=[pl.BlockSpec((ck, hk), rows(C_QC // hk)),
                  pl.BlockSpec((ck, hk), rows(C_KC // hk)),
                  pl.BlockSpec((ck, hv), rows(C_VC // hv)),
                  pl.BlockSpec((ck, hv), rows(C_RC // hv)),
                  pl.BlockSpec((ck, LANE), rows(C_GL // LANE)),
                  pl.BlockSpec((LANE, hk), const),
                  pl.BlockSpec((hk, LANE), const),
                  pl.BlockSpec((1, hk), const),
                  pl.BlockSpec((hk, 1), const),
                  pl.BlockSpec((1, DV_C), const)] + s0_spec,
        out_specs=[pl.BlockSpec((ck, hv), lambda b, c: (b * nc + c, 0)), state],
        out_shape=[jax.ShapeDtypeStruct((nb * seq, hv), BF16),
                   jax.ShapeDtypeStruct((nb, H_C, DK_C, DV_C), F32)],
        compiler_params=_params(("parallel", "arbitrary")),
    )(proj, proj, proj, proj, proj, wgg_pad, wggt_pad, bgg.reshape(1, hk), bgg.reshape(hk, 1),
      gla_g.reshape(1, DV_C), *s0_arg)
    return out, s_fin


def _dsa_body(*refs, tq, lc, lk, lpad, chunked, qt0):
    if lc:
        (q_ref, qi_ref, wi_ref, kn_ref, vn_ref, kan_ref, kbn_ref, kc_ref, vc_ref, kac_ref, kbc_ref,
         o_ref, k_s, v_s, ka_s, kb_s, key_s, u_s) = refs
    else:
        (q_ref, qi_ref, wi_ref, kn_ref, vn_ref, kan_ref, kbn_ref,
         o_ref, k_s, v_s, ka_s, kb_s, key_s, u_s) = refs
    qt = qt0 + pl.program_id(1)
    n_keys = lc + lk

    @pl.when(pl.program_id(1) == 0)
    def _stage_keys():
        if lc:
            k_s[0:lc, :] = kc_ref[...].astype(MXU_DTYPE)
            v_s[0:lc, :] = vc_ref[...].astype(MXU_DTYPE)
            ka_s[0:lc, :] = kac_ref[...].astype(MXU_DTYPE)
            kb_s[0:lc, :] = kbc_ref[...].astype(MXU_DTYPE)
        k_s[lc:n_keys, :] = kn_ref[0:lk, :].astype(MXU_DTYPE)
        v_s[lc:n_keys, :] = vn_ref[0:lk, :].astype(MXU_DTYPE)
        ka_s[lc:n_keys, :] = kan_ref[0:lk, :].astype(MXU_DTYPE)
        kb_s[lc:n_keys, :] = kbn_ref[0:lk, :].astype(MXU_DTYPE)
        if lpad > n_keys:
            k_s[n_keys:lpad, :] = jnp.zeros((lpad - n_keys, N_KV * HD_B), MXU_DTYPE)
            v_s[n_keys:lpad, :] = jnp.zeros((lpad - n_keys, N_KV * HD_B), MXU_DTYPE)
            ka_s[n_keys:lpad, :] = jnp.zeros((lpad - n_keys, LANE), MXU_DTYPE)
            kb_s[n_keys:lpad, :] = jnp.zeros((lpad - n_keys, LANE), MXU_DTYPE)
        for r0 in range(0, lpad, LANE):
            rr = r0 + lax.broadcasted_iota(jnp.int32, (LANE, lpad), 0)
            cc = lax.broadcasted_iota(jnp.int32, (LANE, lpad), 1)
            u_s[r0:r0 + LANE, :] = (rr < cc).astype(MXU_DTYPE)

    qi = qi_ref[...].astype(MXU_DTYPE)
    wi = wi_ref[...] * IDX_W_SCALE
    ka = ka_s[...]
    kb = kb_s[...]
    score = jnp.zeros((tq, lpad), F32)
    for p in range(H_I // 2):
        qp = qi[:, p * LANE:(p + 1) * LANE]
        score = score + wi[:, 2 * p:2 * p + 1] * jnp.maximum(_dot_nt(qp, ka), 0.0)
        score = score + wi[:, 2 * p + 1:2 * p + 2] * jnp.maximum(_dot_nt(qp, kb), 0.0)

    score = jnp.where(score == 0.0, 0.0, score)
    bits = lax.bitcast_convert_type(score, jnp.int32)
    key = bits ^ ((bits >> 31) & 0x7FFFFFFF)
    col = lax.broadcasted_iota(jnp.int32, (tq, lpad), 1)
    if chunked:
        pos = qt * tq + lax.broadcasted_iota(jnp.int32, (tq, lpad), 0)
        valid = col < (pos // CHUNK + 1) * CHUNK
    else:
        valid = col < n_keys
    key_s[...] = jnp.where(valid, key, INT_MIN)

    def count_ge(cand):
        return jnp.sum((key_s[...] >= cand).astype(jnp.int32), axis=1, keepdims=True)

    prefix = jnp.where(count_ge(jnp.zeros((tq, 1), jnp.int32)) >= TOPK, 0, INT_MIN).astype(jnp.int32)

    def search(i, prefix):
        cand = prefix | jnp.left_shift(jnp.int32(1), 30 - i)
        return jnp.where(count_ge(cand) >= TOPK, cand, prefix)

    prefix = lax.fori_loop(0, 31, search, prefix)

    keys = key_s[...]
    above = keys > prefix
    equal = (keys == prefix) & valid
    n_above = jnp.sum(above.astype(jnp.int32), axis=1, keepdims=True)
    rank = _dot(equal.astype(MXU_DTYPE), u_s[...])
    keep = above | (equal & (rank < (TOPK - n_above).astype(F32)))
    bias = jnp.where(keep, 0.0, -jnp.inf)

    q = q_ref[...]
    n_rep = H_B // N_KV
    bias_g = jnp.concatenate([bias] * n_rep, axis=0)
    for n in range(N_KV):
        kn = k_s[:, n * HD_B:(n + 1) * HD_B]
        vn = v_s[:, n * HD_B:(n + 1) * HD_B]
        qg = jnp.concatenate([q[:, (n * n_rep + g) * HD_B:(n * n_rep + g + 1) * HD_B] for g in range(n_rep)], axis=0)
        s = _dot_nt(qg.astype(MXU_DTYPE), kn) * (HD_B ** -0.5) + bias_g
        m = jnp.max(s, axis=1, keepdims=True)
        p = jnp.exp(s - m)
        den = jnp.sum(p, axis=1, keepdims=True)
        o = _dot(p.astype(MXU_DTYPE), vn) / den
        for g in range(n_rep):
            sl = slice((n * n_rep + g) * HD_B, (n * n_rep + g + 1) * HD_B)
            o_ref[:, sl] = o[g * tq:(g + 1) * tq, :].astype(o_ref.dtype)


def _branch_b(proj, row0, nb, seq, tq, chunked, cache=None, layer=0, qt0=0, n_qt=None):
    nq = seq // tq
    n_qt = nq if n_qt is None else n_qt
    rq0 = row0 // tq
    rk0 = row0 // seq
    lc = 0 if cache is None else cache[0].shape[2]
    lk = (qt0 + n_qt) * tq if chunked else seq
    lpad = -(-(lc + lk) // LANE) * LANE
    hk = N_KV * HD_B

    def qrows(cb):
        return lambda b, t: (rq0 + b * nq + qt0 + t, cb)

    def krows(cb):
        return lambda b, t: (rk0 + b, cb)

    in_specs = [pl.BlockSpec((tq, H_B * HD_B), qrows(C_QB // (H_B * HD_B))),
                pl.BlockSpec((tq, H_I * D_I), qrows(C_QI // (H_I * D_I))),
                pl.BlockSpec((tq, LANE), qrows(C_WI // LANE)),
                pl.BlockSpec((seq, hk), krows(C_KB // hk)),
                pl.BlockSpec((seq, hk), krows(C_VB // hk)),
                pl.BlockSpec((seq, LANE), krows(C_KIA // LANE)),
                pl.BlockSpec((seq, LANE), krows(C_KIB // LANE))]
    args = [proj] * 7
    if cache is not None:
        in_specs += [pl.BlockSpec((None, None, lc, hk), lambda b, t: (layer, b, 0, 0)),
                     pl.BlockSpec((None, None, lc, hk), lambda b, t: (layer, b, 0, 0)),
                     pl.BlockSpec((None, None, lc, LANE), lambda b, t: (layer, b, 0, 0)),
                     pl.BlockSpec((None, None, lc, LANE), lambda b, t: (layer, b, 0, 0))]
        args += list(cache)
    return pl.pallas_call(
        functools.partial(_dsa_body, tq=tq, lc=lc, lk=lk, lpad=lpad, chunked=chunked, qt0=qt0),
        grid=(nb, n_qt),
        in_specs=in_specs,
        out_specs=pl.BlockSpec((tq, H_B * HD_B), lambda b, t: (b * n_qt + t, 0)),
        out_shape=jax.ShapeDtypeStruct((nb * n_qt * tq, H_B * HD_B), BF16),
        scratch_shapes=[pltpu.VMEM((lpad, hk), MXU_DTYPE), pltpu.VMEM((lpad, hk), MXU_DTYPE),
                        pltpu.VMEM((lpad, LANE), MXU_DTYPE), pltpu.VMEM((lpad, LANE), MXU_DTYPE),
                        pltpu.VMEM((tq, lpad), jnp.int32), pltpu.VMEM((lpad, lpad), MXU_DTYPE)],
        compiler_params=_params(("parallel", "arbitrary")),
    )(*args)


def _branch_b_prompt(proj, tq=128, tiles_per_call=2):
    nq = SEQ // tq
    bands = [_branch_b(proj, 0, BATCH, SEQ, tq, True, qt0=q0, n_qt=tiles_per_call).reshape(BATCH, -1, H_B * HD_B)
             for q0 in range(0, nq, tiles_per_call)]
    return jnp.concatenate(bands, axis=1).reshape(T_PROMPT, H_B * HD_B)


W_SEG2_SRC, W_SEG3_SRC = 8272, 14432
RELAYOUT_TN = 1024


def _relayout_body(a_ref, b_ref, o_ref, *, tk):
    j = pl.program_id(2)
    n_chunk = RELAYOUT_TN // LANE
    small_tile = C_KIA // RELAYOUT_TN
    rows2 = lax.broadcasted_iota(jnp.int32, (2 * LANE, LANE), 0)
    cols2 = lax.broadcasted_iota(jnp.int32, (2 * LANE, LANE), 1)
    rows1 = lax.broadcasted_iota(jnp.int32, (LANE, LANE), 0)
    cols1 = lax.broadcasted_iota(jnp.int32, (LANE, LANE), 1)

    @pl.when(j != small_tile)
    def _shifted():
        seg2 = (j >= C_QC // RELAYOUT_TN) & (j < small_tile)
        r = jnp.where(j > small_tile, W_SEG3_SRC % LANE, jnp.where(seg2, W_SEG2_SRC % LANE, 0))
        sel = (rows2 == cols2 + r).astype(BF16)
        a = a_ref[...].astype(BF16)
        b = jnp.where(cols1[:1, :] < r, b_ref[...], 0.0).astype(BF16)
        for c in range(n_chunk):
            nxt = a[:, (c + 1) * LANE:(c + 2) * LANE] if c + 1 < n_chunk else b
            pair = jnp.concatenate([a[:, c * LANE:(c + 1) * LANE], nxt], axis=1)
            o_ref[:, c * LANE:(c + 1) * LANE] = _dot(pair, sel).astype(BF16)

    @pl.when(j == small_tile)
    def _small():
        a0 = a_ref[:, 0:LANE].astype(BF16)
        b0 = b_ref[...].astype(BF16)
        ki_a = ((rows1 == cols1) & (cols1 < D_I)).astype(BF16)
        ki_b = ((rows1 + (LANE - D_I) == cols1) & (rows1 < D_I)).astype(BF16)
        wi = ((rows1 == cols1 + D_I) & (cols1 < H_I)).astype(BF16)
        gl = ((rows1 == cols1 + W_SEG3_SRC % LANE - GATE_RANK) & (cols1 < GATE_RANK)).astype(BF16)
        o_ref[:, 0 * LANE:1 * LANE] = _dot(a0, ki_a).astype(BF16)
        o_ref[:, 1 * LANE:2 * LANE] = _dot(a0, ki_b).astype(BF16)
        o_ref[:, 2 * LANE:3 * LANE] = _dot(a0, wi).astype(BF16)
        o_ref[:, 3 * LANE:4 * LANE] = _dot(b0, gl).astype(BF16)
        o_ref[:, 4 * LANE:RELAYOUT_TN] = jnp.zeros((tk, RELAYOUT_TN - 4 * LANE), BF16)


def _relayout_w_in(w_in, tk=1024):
    depth, d, _ = w_in.shape
    tn = RELAYOUT_TN
    small_tile = C_KIA // tn
    seg2_tile0, seg3_tile0 = C_QC // tn, C_GZ // tn
    seg2_a0, seg3_a0 = W_SEG2_SRC // tn, W_SEG3_SRC // tn

    def a_tile(j):
        seg2 = seg2_a0 + (j - seg2_tile0)
        seg3 = seg3_a0 + (j - seg3_tile0)
        return jnp.where(j > small_tile, seg3, jnp.where(j == small_tile, seg2_a0, jnp.where(j >= seg2_tile0, seg2, j)))

    def a_map(l, k, j):
        return l, k, a_tile(j)

    def b_map(l, k, j):
        nxt = (a_tile(j) + 1) * (tn // LANE)
        return l, k, jnp.where(j == small_tile, W_SEG3_SRC // LANE, nxt)

    return pl.pallas_call(
        functools.partial(_relayout_body, tk=tk),
        grid=(depth, d // tk, N_PROJ // tn),
        in_specs=[pl.BlockSpec((None, tk, tn), a_map), pl.BlockSpec((None, tk, LANE), b_map)],
        out_specs=pl.BlockSpec((None, tk, tn), lambda l, k, j: (l, k, j)),
        out_shape=jax.ShapeDtypeStruct((depth, d, N_PROJ), BF16),
        compiler_params=_params(("parallel", "parallel", "arbitrary")),
    )(w_in, w_in)


def kernel(x_prompt, x_sample, cache_k, cache_v, cache_kidx, state_lru, state_conv, state_gla, w_in, conv_w, conv_b, w_rec_gate, b_rec_gate, w_in_gate, b_in_gate, lru_lambda, w_gla_gate, b_gla_gate, gla_norm_g, w_branch, b_branch_gate, w_out, ln1_g, ln1_b, ln2_g, ln2_b, w_ff_gate, w_ff_up, w_ff_down, w_router, w_exp_gate, w_exp_up, w_exp_down):
    hk = N_KV * HD_B
    x = jnp.concatenate([x_prompt.reshape(T_PROMPT, D_MODEL), x_sample.reshape(T_SAMPLE, D_MODEL)], axis=0)
    xb = x.astype(BF16)

    w_in_p = _relayout_w_in(w_in)
    w_ffg = _cast_pad(w_ff_gate, D_MODEL, D_FF_PAD)
    w_ffu = _cast_pad(w_ff_up, D_MODEL, D_FF_PAD)
    w_ffd = _cast_pad(w_ff_down, D_FF_PAD, D_MODEL)
    w_router_p = jnp.pad(w_router, ((0, 0), (0, 0), (0, LANE - N_EXPERTS)))
    wgg_p = jnp.pad(w_gla_gate, ((0, 0), (0, LANE - GATE_RANK), (0, 0)))
    wggt_p = jnp.swapaxes(wgg_p, 1, 2)
    cache_k2 = cache_k.reshape(DEPTH, DEC_BATCH, PAST_LEN, hk)
    cache_v2 = cache_v.reshape(DEPTH, DEC_BATCH, PAST_LEN, hk)
    cache_kia = jnp.pad(cache_kidx, ((0, 0), (0, 0), (0, 0), (0, LANE - D_I)))
    cache_kib = jnp.pad(cache_kidx, ((0, 0), (0, 0), (0, 0), (LANE - D_I, 0)))
    zeros_conv = jnp.zeros((BATCH, CONV_W - 1, W_A), F32)
    zeros_lru = jnp.zeros((BATCH, W_A), F32)

    outs_p = [[] for _ in range(6)]
    outs_s = [[] for _ in range(6)]
    for l in range(DEPTH):
        proj = _matmul(xb, w_in_p, (l,), F32, tm=1024, tn=1024, tk=D_MODEL)

        lru_args = (conv_w[l], conv_b[l], w_rec_gate[l], b_rec_gate[l], w_in_gate[l], b_in_gate[l], lru_lambda[l])
        a_p, lru_p = _branch_a(proj, 0, BATCH, SEQ, 256, zeros_conv, zeros_lru, *lru_args)
        a_s, lru_s = _branch_a(proj, T_PROMPT, DEC_BATCH, DEC_SEQ, DEC_SEQ, state_conv[l], state_lru[l], *lru_args)

        b_p = _branch_b_prompt(proj)
        b_s = _branch_b(proj, T_PROMPT, DEC_BATCH, DEC_SEQ, DEC_SEQ, False,
                        cache=(cache_k2, cache_v2, cache_kia, cache_kib), layer=l)

        gla_args = (wgg_p[l], wggt_p[l], b_gla_gate[l], gla_norm_g[l])
        c_p, gla_p = _branch_c(proj, 0, BATCH, SEQ, 64, None, l, *gla_args)
        c_s, gla_s = _branch_c(proj, T_PROMPT, DEC_BATCH, DEC_SEQ, DEC_SEQ, state_gla, l, *gla_args)

        branches = jnp.stack([jnp.concatenate([a_p, a_s], axis=0), jnp.concatenate([b_p, b_s], axis=0),
                              jnp.concatenate([c_p, c_s], axis=0)])
        mixed = _merge(branches, w_branch, proj, b_branch_gate[l], l)
        y = _matmul(mixed, w_out, (l,), F32, tm=1024, tn=512, tk=D_MODEL)
        x, xb = _ln_residual(x, y, ln1_g[l], ln1_b[l])

        i = l // 2
        if l % 2 == 0:
            h = _glu_dense(xb, w_ffg, w_ffu, i)
            f = _matmul(h, w_ffd, (i,), F32, tm=1024, tn=1024, tk=D_FF_PAD // 4)
        else:
            combine = _router(x, w_router_p[i])
            f = _moe_routed(xb, combine, w_exp_gate, w_exp_up, w_exp_down, i)
        x, xb = _ln_residual(x, f, ln2_g[l], ln2_b[l])

        pp, ps = proj[:T_PROMPT], proj[T_PROMPT:]
        for dst, rows, nb, seq, lru_h, gla_st in ((outs_p, pp, BATCH, SEQ, lru_p, gla_p),
                                                  (outs_s, ps, DEC_BATCH, DEC_SEQ, lru_s, gla_s)):
            dst[0].append(rows[:, C_KB:C_KB + hk].reshape(nb, seq, N_KV, HD_B))
            dst[1].append(rows[:, C_VB:C_VB + hk].reshape(nb, seq, N_KV, HD_B))
            dst[2].append(rows[:, C_KIA:C_KIA + D_I].reshape(nb, seq, D_I))
            dst[3].append(lru_h)
            dst[4].append(rows[:, C_XA:C_XA + W_A].reshape(nb, seq, W_A)[:, seq - (CONV_W - 1):])
            dst[5].append(gla_st)

    k_p, v_p, ki_p, lru_po, conv_p, gla_po = [jnp.stack(o) for o in outs_p]
    k_s, v_s, ki_s, lru_so, conv_s, gla_so = [jnp.stack(o) for o in outs_s]
    return (x[:T_PROMPT].reshape(BATCH, SEQ, D_MODEL), x[T_PROMPT:].reshape(DEC_BATCH, DEC_SEQ, D_MODEL),
            k_p, v_p, ki_p, lru_po, conv_p, gla_po, k_s, v_s, ki_s, lru_so, conv_s, gla_so)
```

```python
import functools

import jax
import jax.numpy as jnp
from jax import lax
from jax.experimental import pallas as pl
from jax.experimental.pallas import tpu as pltpu

F32 = jnp.float32
BF16 = jnp.bfloat16
MXU_DTYPE = BF16
HIGHEST = lax.Precision.HIGHEST

D_MODEL = 4096
BATCH, SEQ = 4, 2048
DEPTH = 4
DEC_BATCH, DEC_SEQ = 32, 32
PAST_LEN = 1024
T_PROMPT = BATCH * SEQ
T_SAMPLE = DEC_BATCH * DEC_SEQ
T_ALL = T_PROMPT + T_SAMPLE
CHUNK = 64
W_BRANCH = 2048
W_A = W_BRANCH
NB_A = 16
BW_A = W_A // NB_A
CONV_W = 4
LRU_C = 8.0
H_B, HD_B, N_KV = 16, 128, 4
H_I, D_I = 16, 64
TOPK = 256
IDX_W_SCALE = (H_I ** -0.5) * (D_I ** -0.5)
H_C, DK_C, DV_C = 4, 256, 512
GATE_RANK = 16
GATE_NORM = 16.0
N_BRANCH = 3
D_FF = 11008
D_FF_PAD = 11264
N_EXPERTS = 8
D_FF_E = 7168
ALPHA = (2.0 * DEPTH) ** 0.25
LN_EPS = 1e-5
RMS_EPS = 1e-6
LANE = 128
INT_MIN = -2 ** 31

C_XA, C_GA, C_QB, C_KB, C_VB, C_QI = 0, 2048, 4096, 6144, 6656, 7168
C_QC, C_KC, C_VC, C_RC = 8192, 9216, 10240, 12288
C_KIA, C_KIB, C_WI, C_GL = 14336, 14464, 14592, 14720
C_GZ = 15360
N_PROJ = C_GZ + N_BRANCH * D_MODEL

VMEM_LIMIT = 56 * 1024 * 1024


def _params(sem):
    return pltpu.CompilerParams(dimension_semantics=sem, vmem_limit_bytes=VMEM_LIMIT)


def _sigmoid(x):
    return 1.0 / (1.0 + jnp.exp(-x))


def _log_sigmoid(x):
    return jnp.minimum(x, 0.0) - jnp.log(1.0 + jnp.exp(-jnp.abs(x)))


def _dot(a, b):
    return jnp.dot(a, b, preferred_element_type=F32)


def _dot_nt(a, b):
    return lax.dot_general(a, b, (((1,), (1,)), ((), ())), preferred_element_type=F32)


def _dot_tn(a, b):
    return lax.dot_general(a, b, (((0,), (0,)), ((), ())), preferred_element_type=F32)


def _mm_body(x_ref, w_ref, o_ref, acc_ref):
    k = pl.program_id(2)

    @pl.when(k == 0)
    def _init():
        acc_ref[...] = jnp.zeros_like(acc_ref)

    acc_ref[...] += _dot(x_ref[...].astype(MXU_DTYPE), w_ref[...].astype(MXU_DTYPE))

    @pl.when(k == pl.num_programs(2) - 1)
    def _fin():
        o_ref[...] = acc_ref[...].astype(o_ref.dtype)


def _mm_full_k_body(x_ref, w_ref, o_ref):
    o_ref[...] = _dot(x_ref[...].astype(MXU_DTYPE), w_ref[...].astype(MXU_DTYPE)).astype(o_ref.dtype)


def _matmul(x, w, lead, out_dtype, tm, tn, tk):
    m_dim, k_dim = x.shape
    n_dim = w.shape[-1]
    nl = len(lead)
    full_k = tk == k_dim
    return pl.pallas_call(
        _mm_full_k_body if full_k else _mm_body,
        grid=(m_dim // tm, n_dim // tn, k_dim // tk),
        in_specs=[pl.BlockSpec((tm, tk), lambda m, n, k: (m, k)),
                  pl.BlockSpec((None,) * nl + (tk, tn), lambda m, n, k: lead + (k, n))],
        out_specs=pl.BlockSpec((tm, tn), lambda m, n, k: (m, n)),
        out_shape=jax.ShapeDtypeStruct((m_dim, n_dim), out_dtype),
        scratch_shapes=[] if full_k else [pltpu.VMEM((tm, tn), F32)],
        compiler_params=_params(("parallel", "parallel", "arbitrary")),
    )(x, w)


def _cast_pad_body(x_ref, o_ref, *, rows, cols):
    tr, tc = o_ref.shape
    r = pl.program_id(1) * tr + lax.broadcasted_iota(jnp.int32, (tr, tc), 0)
    c = pl.program_id(2) * tc + lax.broadcasted_iota(jnp.int32, (tr, tc), 1)
    o_ref[...] = jnp.where((r < rows) & (c < cols), x_ref[...], 0.0).astype(o_ref.dtype)


def _cast_pad(w, rows_pad, cols_pad, tr=512, tc=1024):
    n_l, rows, cols = w.shape
    assert (rows_pad - rows) < tr and (cols_pad - cols) < tc
    spec = pl.BlockSpec((None, tr, tc), lambda l, i, j: (l, i, j))
    return pl.pallas_call(
        functools.partial(_cast_pad_body, rows=rows, cols=cols),
        grid=(n_l, rows_pad // tr, cols_pad // tc),
        in_specs=[spec], out_specs=spec,
        out_shape=jax.ShapeDtypeStruct((n_l, rows_pad, cols_pad), BF16),
        compiler_params=_params(("parallel", "parallel", "parallel")),
    )(w)


def _ln_body(x_ref, y_ref, g_ref, b_ref, o_ref, ob_ref):
    s = ALPHA * x_ref[...] + y_ref[...]
    mu = jnp.mean(s, axis=-1, keepdims=True)
    d = s - mu
    var = jnp.mean(d * d, axis=-1, keepdims=True)
    o = d * lax.rsqrt(var + LN_EPS) * g_ref[...] + b_ref[...]
    o_ref[...] = o
    ob_ref[...] = o.astype(BF16)


def _ln_residual(x, y, g, b, tm=256):
    t_dim = x.shape[0]
    row = pl.BlockSpec((tm, D_MODEL), lambda m: (m, 0))
    vec = pl.BlockSpec((1, D_MODEL), lambda m: (0, 0))
    return pl.pallas_call(
        _ln_body,
        grid=(t_dim // tm,),
        in_specs=[row, row, vec, vec],
        out_specs=[row, row],
        out_shape=[jax.ShapeDtypeStruct((t_dim, D_MODEL), F32), jax.ShapeDtypeStruct((t_dim, D_MODEL), BF16)],
        compiler_params=_params(("parallel",)),
    )(x, y, g.reshape(1, D_MODEL), b.reshape(1, D_MODEL))


def _glu_body(x_ref, wg_ref, wu_ref, o_ref):
    x = x_ref[...]
    g = _dot(x, wg_ref[...].astype(MXU_DTYPE))
    u = _dot(x, wu_ref[...].astype(MXU_DTYPE))
    o_ref[...] = (g * _sigmoid(g) * u).astype(o_ref.dtype)


def _glu_dense(xb, wg, wu, layer, tm=1024, tn=512):
    t_dim, k_dim = xb.shape
    n_dim = wg.shape[-1]
    wspec = pl.BlockSpec((None, k_dim, tn), lambda m, n: (layer, 0, n))
    return pl.pallas_call(
        _glu_body,
        grid=(t_dim // tm, n_dim // tn),
        in_specs=[pl.BlockSpec((tm, k_dim), lambda m, n: (m, 0)), wspec, wspec],
        out_specs=pl.BlockSpec((tm, tn), lambda m, n: (m, n)),
        out_shape=jax.ShapeDtypeStruct((t_dim, n_dim), BF16),
        compiler_params=_params(("parallel", "parallel")),
    )(xb, wg, wu)


MOE_TM = 1024
MOE_G = 256
MOE_SRC = 512


def _moe_plan(combine, n_exp, tm, g, src):
    i32 = jnp.int32
    t_dim = combine.shape[0]
    n_rows = 2 * t_dim + n_exp * tm
    mask = combine[:, :n_exp] > 0.0
    mi = mask.astype(i32)
    cum = jnp.cumsum(mi, axis=0)
    cnt = cum[-1]
    gs = (cnt + tm - 1) // tm * tm
    g_end = jnp.cumsum(gs)
    g0 = g_end - gs
    rowid = jnp.where(mask, g0[None, :] + cum - mi, -1).astype(i32)
    used_rows = g_end[-1]

    n_mt = n_rows // tm
    mt_used = used_rows // tm
    mt_row0 = jnp.arange(n_mt, dtype=i32) * tm
    mt_exp = jnp.minimum(jnp.searchsorted(g_end, mt_row0, side="right"), n_exp - 1)
    mt_live = jnp.clip(cnt[mt_exp] - (mt_row0 - g0[mt_exp]), 0, tm)

    n_gt = n_rows // g
    n_sb = t_dim // src
    gt_used = used_rows // g
    r0 = jnp.arange(n_gt, dtype=i32) * g
    gt_exp = jnp.minimum(jnp.searchsorted(g_end, r0, side="right"), n_exp - 1).astype(i32)
    rank0 = r0 - g0[gt_exp]
    rank1 = jnp.minimum(rank0 + g, cnt[gt_exp]) - 1
    cb = cum[src - 1::src].T
    cb_t = cb[gt_exp]
    fb = jnp.sum(cb_t <= rank0[:, None], axis=1)
    lb = jnp.sum(cb_t <= rank1[:, None], axis=1)
    has_rows = rank1 >= rank0
    fb = jnp.where(has_rows, jnp.minimum(fb, n_sb - 1), 0)
    lb = jnp.where(has_rows, jnp.minimum(lb, n_sb - 1), 0)
    nblk = jnp.where(jnp.arange(n_gt) < gt_used, lb - fb + 1, 0)
    off_end = jnp.cumsum(nblk)
    off = off_end - nblk
    n_items = n_gt + n_exp * n_sb
    w = jnp.arange(n_items, dtype=i32)
    it_valid = w < off_end[-1]
    it_tile = jnp.minimum(jnp.searchsorted(off_end, w, side="right"), jnp.maximum(gt_used - 1, 0)).astype(i32)
    it_blk = jnp.clip(fb[it_tile] + w - off[it_tile], 0, lb[it_tile]).astype(i32)
    it_first = (w == off[it_tile])
    gather = jnp.stack([it_tile, it_blk, gt_exp[it_tile], it_first.astype(i32), it_valid.astype(i32)])

    ce = cum[g - 1::g]
    cs = jnp.concatenate([jnp.zeros((1, n_exp), i32), ce[:-1]], axis=0)
    a = g0[None, :] + cs
    b = g0[None, :] + ce
    blk0 = jnp.clip(a // g, 0, n_gt - 1)
    blk1 = jnp.clip((b - 1) // g, 0, n_gt - 1)
    v0 = b > a
    v1 = v0 & (blk1 > blk0)
    c_blk = jnp.stack([blk0, jnp.where(v1, blk1, blk0)], axis=-1).reshape(-1).astype(i32)
    c_val = jnp.stack([v0, v1], axis=-1).reshape(-1).astype(i32)
    return dict(n_rows=n_rows, rowid=rowid, mt_used=mt_used.reshape(1).astype(i32), mt_exp=mt_exp.astype(i32),
                mt_live=mt_live.astype(i32), gather=gather, c_blk=c_blk, c_val=c_val)


def _moe_gather_body(it_ref, x_ref, rid_ref, o_ref, *, g):
    w = pl.program_id(0)
    tile, e, first, valid = it_ref[0, w], it_ref[2, w], it_ref[3, w], it_ref[4, w]

    @pl.when(valid == 1)
    def _():
        rid = rid_ref[pl.ds(e, 1), :]
        rows = tile * g + lax.broadcasted_iota(jnp.int32, (g, rid.shape[1]), 0)
        part = _dot((rid == rows).astype(MXU_DTYPE), x_ref[...]).astype(o_ref.dtype)

        @pl.when(first == 1)
        def _set():
            o_ref[...] = part

        @pl.when(first == 0)
        def _add():
            o_ref[...] += part


def _moe_gather(xb, rowid_t, plan, g, src):
    d = xb.shape[1]
    items = plan["gather"]
    return pl.pallas_call(
        functools.partial(_moe_gather_body, g=g),
        grid_spec=pltpu.PrefetchScalarGridSpec(
            num_scalar_prefetch=1, grid=(items.shape[1],),
            in_specs=[pl.BlockSpec((src, d), lambda w, it: (it[1, w], 0)),
                      pl.BlockSpec((rowid_t.shape[0], src), lambda w, it: (0, it[1, w]))],
            out_specs=pl.BlockSpec((g, d), lambda w, it: (it[0, w], 0))),
        out_shape=jax.ShapeDtypeStruct((plan["n_rows"], d), xb.dtype),
        compiler_params=_params(("arbitrary",)),
    )(items, xb, rowid_t)


def _moe_up_body(used_ref, exp_ref, live_ref, x_ref, wg_ref, wu_ref, o_ref):
    i = pl.program_id(0)
    half = x_ref.shape[0] // 2

    def run(rows):
        x = x_ref[0:rows, :]
        gate = _dot(x, wg_ref[...].astype(MXU_DTYPE))
        up = _dot(x, wu_ref[...].astype(MXU_DTYPE))
        o_ref[0:rows, :] = (gate * _sigmoid(gate) * up).astype(o_ref.dtype)

    @pl.when((i < used_ref[0]) & (live_ref[i] > half))
    def _full():
        run(2 * half)

    @pl.when((i < used_ref[0]) & (live_ref[i] <= half))
    def _half():
        run(half)


def _grouped_index(nn, nk):
    def pick(i, n, k, used):
        live = i < used[0]
        last = jnp.maximum(used[0] - 1, 0)
        return jnp.where(live, i, last), jnp.where(live, n, nn - 1), jnp.where(live, k, nk - 1)
    return pick


def _moe_up(xg, wg, wu, layer, plan, tm, tn):
    n_rows, d = xg.shape
    f = wg.shape[-1]
    nn = f // tn
    pick = _grouped_index(nn, 1)

    def x_map(i, n, used, exp, live):
        return pick(i, n, 0, used)[0], 0

    def w_map(i, n, used, exp, live):
        ii, n2, _ = pick(i, n, 0, used)
        return layer, exp[ii], 0, n2

    def o_map(i, n, used, exp, live):
        return pick(i, n, 0, used)[:2]

    wspec = pl.BlockSpec((None, None, d, tn), w_map)
    return pl.pallas_call(
        _moe_up_body,
        grid_spec=pltpu.PrefetchScalarGridSpec(
            num_scalar_prefetch=3, grid=(n_rows // tm, nn),
            in_specs=[pl.BlockSpec((tm, d), x_map), wspec, wspec],
            out_specs=pl.BlockSpec((tm, tn), o_map)),
        out_shape=jax.ShapeDtypeStruct((n_rows, f), xg.dtype),
        compiler_params=_params(("arbitrary", "arbitrary")),
    )(plan["mt_used"], plan["mt_exp"], plan["mt_live"], xg, wg, wu)


def _moe_down_body(used_ref, exp_ref, live_ref, x_ref, w_ref, hi_ref, lo_ref):
    i = pl.program_id(0)
    half = x_ref.shape[0] // 2

    def run(rows):
        y = _dot(x_ref[0:rows, :], w_ref[...].astype(MXU_DTYPE))
        hi = y.astype(hi_ref.dtype)
        hi_ref[0:rows, :] = hi
        lo_ref[0:rows, :] = (y - hi.astype(F32)).astype(lo_ref.dtype)

    @pl.when((i < used_ref[0]) & (live_ref[i] > half))
    def _full():
        run(2 * half)

    @pl.when((i < used_ref[0]) & (live_ref[i] <= half))
    def _half():
        run(half)


def _moe_down(h, wd, layer, plan, tm, tn):
    n_rows, f = h.shape
    d = wd.shape[-1]
    nn = d // tn
    pick = _grouped_index(nn, 1)

    def x_map(i, n, used, exp, live):
        return pick(i, n, 0, used)[0], 0

    def w_map(i, n, used, exp, live):
        ii, n2, _ = pick(i, n, 0, used)
        return layer, exp[ii], 0, n2

    def o_map(i, n, used, exp, live):
        return pick(i, n, 0, used)[:2]

    piece = jax.ShapeDtypeStruct((n_rows, d), h.dtype)
    return pl.pallas_call(
        _moe_down_body,
        grid_spec=pltpu.PrefetchScalarGridSpec(
            num_scalar_prefetch=3, grid=(n_rows // tm, nn),
            in_specs=[pl.BlockSpec((tm, f), x_map), pl.BlockSpec((None, None, f, tn), w_map)],
            out_specs=[pl.BlockSpec((tm, tn), o_map), pl.BlockSpec((tm, tn), o_map)]),
        out_shape=[piece, piece],
        compiler_params=_params(("arbitrary", "arbitrary")),
    )(plan["mt_used"], plan["mt_exp"], plan["mt_live"], h, wd)


def _moe_combine_body(blk_ref, val_ref, hi_ref, lo_ref, rid_ref, c_ref, o_ref, *, g, n_exp):
    m, e, s = pl.program_id(0), pl.program_id(1), pl.program_id(2)
    item = (m * n_exp + e) * 2 + s

    @pl.when((e == 0) & (s == 0))
    def _init():
        o_ref[...] = jnp.zeros_like(o_ref)

    @pl.when(val_ref[item] == 1)
    def _():
        rid = rid_ref[...]
        rid_e = jnp.sum(jnp.where(lax.broadcasted_iota(jnp.int32, rid.shape, 1) == e, rid, 0), axis=1, keepdims=True)
        c = c_ref[...]
        c_e = jnp.sum(jnp.where(lax.broadcasted_iota(jnp.int32, c.shape, 1) == e, c, 0.0), axis=1, keepdims=True)
        rows = blk_ref[item] * g + lax.broadcasted_iota(jnp.int32, (g, g), 1)
        onehot = (rid_e == rows).astype(MXU_DTYPE)
        o_ref[...] += c_e * (_dot(onehot, hi_ref[...]) + _dot(onehot, lo_ref[...]))


def _moe_combine(y_hi, y_lo, rowid, combine, plan, g):
    t_dim, n_exp = rowid.shape
    d = y_hi.shape[1]

    def y_map(m, e, s, blk, val):
        return blk[(m * n_exp + e) * 2 + s], 0

    return pl.pallas_call(
        functools.partial(_moe_combine_body, g=g, n_exp=n_exp),
        grid_spec=pltpu.PrefetchScalarGridSpec(
            num_scalar_prefetch=2, grid=(t_dim // g, n_exp, 2),
            in_specs=[pl.BlockSpec((g, d), y_map), pl.BlockSpec((g, d), y_map),
                      pl.BlockSpec((g, n_exp), lambda m, e, s, blk, val: (m, 0)),
                      pl.BlockSpec((g, combine.shape[1]), lambda m, e, s, blk, val: (m, 0))],
            out_specs=pl.BlockSpec((g, d), lambda m, e, s, blk, val: (m, 0))),
        out_shape=jax.ShapeDtypeStruct((t_dim, d), F32),
        compiler_params=_params(("arbitrary", "arbitrary", "arbitrary")),
    )(plan["c_blk"], plan["c_val"], y_hi, y_lo, rowid, combine)


def _moe_routed(xb, combine, wg, wu, wd, layer, tm=MOE_TM, g=MOE_G, src=MOE_SRC, tn_up=256, tn_down=256):
    n_exp = wg.shape[1]
    plan = _moe_plan(combine, n_exp, tm, g, src)
    xg = _moe_gather(xb, plan["rowid"].T, plan, g, src)
    h = _moe_up(xg, wg, wu, layer, plan, tm, tn_up)
    y_hi, y_lo = _moe_down(h, wd, layer, plan, tm, tn_down)
    return _moe_combine(y_hi, y_lo, plan["rowid"], combine, plan, g)


def _router_body(x_ref, w_ref, c_ref):
    logits = jnp.dot(x_ref[...], w_ref[...], preferred_element_type=F32, precision=HIGHEST)
    lane = lax.broadcasted_iota(jnp.int32, logits.shape, 1)
    logits = jnp.where(lane < N_EXPERTS, logits, -jnp.inf)
    m1 = jnp.max(logits, axis=1, keepdims=True)
    i1 = jnp.min(jnp.where(logits == m1, lane, LANE), axis=1, keepdims=True)
    rest = jnp.where(lane == i1, -jnp.inf, logits)
    m2 = jnp.max(rest, axis=1, keepdims=True)
    i2 = jnp.min(jnp.where(rest == m2, lane, LANE), axis=1, keepdims=True)
    e2 = jnp.exp(m2 - m1)
    w1 = 1.0 / (1.0 + e2)
    w2 = e2 / (1.0 + e2)
    c_ref[...] = jnp.where(lane == i1, w1, 0.0) + jnp.where(lane == i2, w2, 0.0)


def _router(x, w_pad, tm=512):
    t_dim = x.shape[0]
    return pl.pallas_call(
        _router_body,
        grid=(t_dim // tm,),
        in_specs=[pl.BlockSpec((tm, D_MODEL), lambda m: (m, 0)),
                  pl.BlockSpec((D_MODEL, LANE), lambda m: (0, 0))],
        out_specs=pl.BlockSpec((tm, LANE), lambda m: (m, 0)),
        out_shape=jax.ShapeDtypeStruct((t_dim, LANE), F32),
        compiler_params=_params(("parallel",)),
    )(x, w_pad)


def _merge_body(br_ref, w_ref, gz_ref, bg_ref, o_ref, acc_ref):
    j = pl.program_id(2)

    @pl.when(j == 0)
    def _init():
        acc_ref[...] = jnp.zeros_like(acc_ref)

    pj = _dot(br_ref[...].astype(MXU_DTYPE), w_ref[...].astype(MXU_DTYPE))
    acc_ref[...] += _sigmoid(gz_ref[...] + bg_ref[...]) * pj

    @pl.when(j == N_BRANCH - 1)
    def _fin():
        o_ref[...] = acc_ref[...].astype(o_ref.dtype)


def _merge(branches, w_branch, proj, b_gate, layer, tm=1024, tn=1024):
    nt = D_MODEL // tn
    gz_blk = C_GZ // tn
    return pl.pallas_call(
        _merge_body,
        grid=(T_ALL // tm, nt, N_BRANCH),
        in_specs=[pl.BlockSpec((None, tm, W_BRANCH), lambda m, n, j: (j, m, 0)),
                  pl.BlockSpec((None, None, W_BRANCH, tn), lambda m, n, j: (layer, j, 0, n)),
                  pl.BlockSpec((tm, tn), lambda m, n, j: (m, gz_blk + j * nt + n)),
                  pl.BlockSpec((1, tn), lambda m, n, j: (0, j * nt + n))],
        out_specs=pl.BlockSpec((tm, tn), lambda m, n, j: (m, n)),
        out_shape=jax.ShapeDtypeStruct((T_ALL, D_MODEL), BF16),
        scratch_shapes=[pltpu.VMEM((tm, tn), F32)],
        compiler_params=_params(("parallel", "parallel", "arbitrary")),
    )(branches, w_branch, proj, b_gate.reshape(1, N_BRANCH * D_MODEL))


def _lru_body(xa_ref, ga_ref, buf_ref, h0_ref, cw_ref, cb_ref, wrg_ref, brg_ref, wig_ref, big_ref, lam_ref,
              o_ref, hlast_ref, xp_ref, a_ref, u_ref, h_ref, *, tl):
    t = pl.program_id(1)

    @pl.when(t == 0)
    def _init():
        xp_ref[5:8, :] = buf_ref[...]
        h_ref[...] = h0_ref[...]

    xp_ref[8:8 + tl, :] = xa_ref[...]
    xc = cb_ref[...] + xp_ref[8:8 + tl, :] * cw_ref[3:4, :]
    for j in range(CONV_W - 1):
        xc = xc + xp_ref[5 + j:5 + j + tl, :] * cw_ref[j:j + 1, :]
    xp_ref[5:8, :] = xa_ref[tl - 3:tl, :]

    xcb = xc.astype(MXU_DTYPE)
    for n in range(NB_A):
        sl = slice(n * BW_A, (n + 1) * BW_A)
        xs = xcb[:, sl]
        r = _sigmoid(_dot(xs, wrg_ref[n].astype(MXU_DTYPE)) + brg_ref[:, sl])
        i = _sigmoid(_dot(xs, wig_ref[n].astype(MXU_DTYPE)) + big_ref[:, sl])
        lam = lam_ref[:, sl]
        softplus = jnp.maximum(-lam, 0.0) + jnp.log(1.0 + jnp.exp(-jnp.abs(lam)))
        log_a = -LRU_C * r * softplus
        a_ref[:, sl] = jnp.exp(log_a)
        u_ref[:, sl] = jnp.sqrt(1.0 - jnp.exp(2.0 * log_a)) * (i * xc[:, sl])

    def step(s, h):
        h = a_ref[pl.ds(s, 1), :] * h + u_ref[pl.ds(s, 1), :]
        a_ref[pl.ds(s, 1), :] = h
        return h

    h = lax.fori_loop(0, tl, step, h_ref[...], unroll=8)
    h_ref[...] = h
    hlast_ref[...] = h
    g = ga_ref[...]
    gelu = 0.5 * g * (1.0 + jnp.tanh(0.7978845608028654 * (g + 0.044715 * (g * g * g))))
    o_ref[...] = (a_ref[...] * gelu).astype(o_ref.dtype)


def _branch_a(proj, row0, nb, seq, tl, conv_buf, h0, cw, cb, wrg, brg, wig, big, lam):
    nt = seq // tl
    rb0 = row0 // tl
    vec = pl.BlockSpec((1, W_A), lambda b, t: (0, 0))
    blk = pl.BlockSpec((NB_A, BW_A, BW_A), lambda b, t: (0, 0, 0))
    out, h_last = pl.pallas_call(
        functools.partial(_lru_body, tl=tl),
        grid=(nb, nt),
        in_specs=[pl.BlockSpec((tl, W_A), lambda b, t: (rb0 + b * nt + t, C_XA // W_A)),
                  pl.BlockSpec((tl, W_A), lambda b, t: (rb0 + b * nt + t, C_GA // W_A)),
                  pl.BlockSpec((None, CONV_W - 1, W_A), lambda b, t: (b, 0, 0)),
                  pl.BlockSpec((None, 1, W_A), lambda b, t: (b, 0, 0)),
                  pl.BlockSpec((CONV_W, W_A), lambda b, t: (0, 0)),
                  vec, blk, vec, blk, vec, vec],
        out_specs=[pl.BlockSpec((tl, W_A), lambda b, t: (b * nt + t, 0)),
                   pl.BlockSpec((None, 1, W_A), lambda b, t: (b, 0, 0))],
        out_shape=[jax.ShapeDtypeStruct((nb * seq, W_A), BF16), jax.ShapeDtypeStruct((nb, 1, W_A), F32)],
        scratch_shapes=[pltpu.VMEM((tl + 8, W_A), F32), pltpu.VMEM((tl, W_A), F32), pltpu.VMEM((tl, W_A), F32),
                        pltpu.VMEM((1, W_A), F32)],
        compiler_params=_params(("parallel", "arbitrary")),
    )(proj, proj, conv_buf, h0.reshape(nb, 1, W_A), cw, cb.reshape(1, W_A), wrg, brg.reshape(1, W_A),
      wig, big.reshape(1, W_A), lam.reshape(1, W_A))
    return out, h_last.reshape(nb, W_A)


def _gla_body(*refs, ck, has_s0):
    if has_s0:
        q_ref, k_ref, v_ref, rc_ref, gl_ref, wgg_ref, wggt_ref, bgr_ref, bgc_ref, g_ref, s0_ref, o_ref, s_ref = refs
    else:
        q_ref, k_ref, v_ref, rc_ref, gl_ref, wgg_ref, wggt_ref, bgr_ref, bgc_ref, g_ref, o_ref, s_ref = refs
    c = pl.program_id(1)

    @pl.when(c == 0)
    def _init():
        s_ref[...] = s0_ref[...] if has_s0 else jnp.zeros_like(s_ref)

    gl = gl_ref[...].astype(MXU_DTYPE)
    lg = _log_sigmoid(_dot(gl, wgg_ref[...].astype(MXU_DTYPE)) + bgr_ref[...]) / GATE_NORM
    lg_t = _log_sigmoid(_dot_nt(wggt_ref[...].astype(MXU_DTYPE), gl) + bgc_ref[...]) / GATE_NORM
    row = lax.broadcasted_iota(jnp.int32, (ck, ck), 0)
    col = lax.broadcasted_iota(jnp.int32, (ck, ck), 1)
    tri = row >= col
    bcum_all = jnp.dot(tri.astype(F32), lg, preferred_element_type=F32, precision=HIGHEST)
    b_last_col_all = jnp.sum(lg_t, axis=1, keepdims=True)

    for h in range(H_C):
        ks = slice(h * DK_C, (h + 1) * DK_C)
        vs = slice(h * DV_C, (h + 1) * DV_C)
        bcum = bcum_all[:, ks]
        b_last = bcum[ck - 1:ck, :]
        k = k_ref[:, ks]
        qe = (q_ref[:, ks] * (DK_C ** -0.5) * jnp.exp(bcum)).astype(MXU_DTYPE)
        ke = (k * jnp.exp(-bcum)).astype(MXU_DTYPE)
        kt = (k * jnp.exp(b_last - bcum)).astype(MXU_DTYPE)
        vb = v_ref[:, vs].astype(MXU_DTYPE)
        att = jnp.where(tri, _dot_nt(qe, ke), 0.0)
        s = s_ref[h]
        o = _dot(att.astype(MXU_DTYPE), vb) + _dot(qe, s.astype(MXU_DTYPE))
        s_ref[h] = jnp.exp(b_last_col_all[ks, :]) * s + _dot_tn(kt, vb)

        o = o * lax.rsqrt(jnp.mean(o * o, axis=-1, keepdims=True) + RMS_EPS) * g_ref[...]
        rc = rc_ref[:, vs]
        o_ref[:, vs] = (o * (rc * _sigmoid(rc))).astype(o_ref.dtype)


def _branch_c(proj, row0, nb, seq, ck, s0, layer, wgg_pad, wggt_pad, bgg, gla_g):
    nc = seq // ck
    rb0 = row0 // ck
    hk, hv = H_C * DK_C, H_C * DV_C

    def rows(cb):
        return lambda b, c: (rb0 + b * nc + c, cb)

    def const(b, c):
        return (0, 0)

    state = pl.BlockSpec((None, H_C, DK_C, DV_C), lambda b, c: (b, 0, 0, 0))
    has_s0 = s0 is not None
    s0_spec = [pl.BlockSpec((None, None, H_C, DK_C, DV_C), lambda b, c: (layer, b, 0, 0, 0))] if has_s0 else []
    s0_arg = [s0] if has_s0 else []
    out, s_fin = pl.pallas_call(
        functools.partial(_gla_body, ck=ck, has_s0=has_s0),
        grid=(nb, nc),
        in_specs=[pl.BlockSpec((ck, hk), rows(C_QC // hk)),
                  pl.BlockSpec((ck, hk), rows(C_KC // hk)),
                  pl.BlockSpec((ck, hv), rows(C_VC // hv)),
                  pl.BlockSpec((ck, hv), rows(C_RC // hv)),
                  pl.BlockSpec((ck, LANE), rows(C_GL // LANE)),
                  pl.BlockSpec((LANE, hk), const),
                  pl.BlockSpec((hk, LANE), const),
                  pl.BlockSpec((1, hk), const),
                  pl.BlockSpec((hk, 1), const),
                  pl.BlockSpec((1, DV_C), const)] + s0_spec,
        out_specs=[pl.BlockSpec((ck, hv), lambda b, c: (b * nc + c, 0)), state],
        out_shape=[jax.ShapeDtypeStruct((nb * seq, hv), BF16),
                   jax.ShapeDtypeStruct((nb, H_C, DK_C, DV_C), F32)],
        compiler_params=_params(("parallel", "arbitrary")),
    )(proj, proj, proj, proj, proj, wgg_pad, wggt_pad, bgg.reshape(1, hk), bgg.reshape(hk, 1),
      gla_g.reshape(1, DV_C), *s0_arg)
    return out, s_fin


def _dsa_body(*refs, tq, lc, lk, lpad, chunked, qt0):
    if lc:
        (q_ref, qi_ref, wi_ref, kn_ref, vn_ref, kan_ref, kbn_ref, kc_ref, vc_ref, kac_ref, kbc_ref,
         o_ref, k_s, v_s, ka_s, kb_s, key_s, u_s) = refs
    else:
        (q_ref, qi_ref, wi_ref, kn_ref, vn_ref, kan_ref, kbn_ref,
         o_ref, k_s, v_s, ka_s, kb_s, key_s, u_s) = refs
    qt = qt0 + pl.program_id(1)
    n_keys = lc + lk

    @pl.when(pl.program_id(1) == 0)
    def _stage_keys():
        if lc:
            k_s[0:lc, :] = kc_ref[...].astype(MXU_DTYPE)
            v_s[0:lc, :] = vc_ref[...].astype(MXU_DTYPE)
            ka_s[0:lc, :] = kac_ref[...].astype(MXU_DTYPE)
            kb_s[0:lc, :] = kbc_ref[...].astype(MXU_DTYPE)
        k_s[lc:n_keys, :] = kn_ref[0:lk, :].astype(MXU_DTYPE)
        v_s[lc:n_keys, :] = vn_ref[0:lk, :].astype(MXU_DTYPE)
        ka_s[lc:n_keys, :] = kan_ref[0:lk, :].astype(MXU_DTYPE)
        kb_s[lc:n_keys, :] = kbn_ref[0:lk, :].astype(MXU_DTYPE)
        if lpad > n_keys:
            k_s[n_keys:lpad, :] = jnp.zeros((lpad - n_keys, N_KV * HD_B), MXU_DTYPE)
            v_s[n_keys:lpad, :] = jnp.zeros((lpad - n_keys, N_KV * HD_B), MXU_DTYPE)
            ka_s[n_keys:lpad, :] = jnp.zeros((lpad - n_keys, LANE), MXU_DTYPE)
            kb_s[n_keys:lpad, :] = jnp.zeros((lpad - n_keys, LANE), MXU_DTYPE)
        for r0 in range(0, lpad, LANE):
            rr = r0 + lax.broadcasted_iota(jnp.int32, (LANE, lpad), 0)
            cc = lax.broadcasted_iota(jnp.int32, (LANE, lpad), 1)
            u_s[r0:r0 + LANE, :] = (rr < cc).astype(MXU_DTYPE)

    qi = qi_ref[...].astype(MXU_DTYPE)
    wi = wi_ref[...] * IDX_W_SCALE
    ka = ka_s[...]
    kb = kb_s[...]
    score = jnp.zeros((tq, lpad), F32)
    for p in range(H_I // 2):
        qp = qi[:, p * LANE:(p + 1) * LANE]
        score = score + wi[:, 2 * p:2 * p + 1] * jnp.maximum(_dot_nt(qp, ka), 0.0)
        score = score + wi[:, 2 * p + 1:2 * p + 2] * jnp.maximum(_dot_nt(qp, kb), 0.0)

    score = jnp.where(score == 0.0, 0.0, score)
    bits = lax.bitcast_convert_type(score, jnp.int32)
    key = bits ^ ((bits >> 31) & 0x7FFFFFFF)
    col = lax.broadcasted_iota(jnp.int32, (tq, lpad), 1)
    if chunked:
        pos = qt * tq + lax.broadcasted_iota(jnp.int32, (tq, lpad), 0)
        valid = col < (pos // CHUNK + 1) * CHUNK
    else:
        valid = col < n_keys
    key_s[...] = jnp.where(valid, key, INT_MIN)

    def count_ge(cand):
        return jnp.sum((key_s[...] >= cand).astype(jnp.int32), axis=1, keepdims=True)

    prefix = jnp.where(count_ge(jnp.zeros((tq, 1), jnp.int32)) >= TOPK, 0, INT_MIN).astype(jnp.int32)

    def search(i, prefix):
        cand = prefix | jnp.left_shift(jnp.int32(1), 30 - i)
        return jnp.where(count_ge(cand) >= TOPK, cand, prefix)

    prefix = lax.fori_loop(0, 31, search, prefix)

    keys = key_s[...]
    above = keys > prefix
    equal = (keys == prefix) & valid
    n_above = jnp.sum(above.astype(jnp.int32), axis=1, keepdims=True)
    rank = _dot(equal.astype(MXU_DTYPE), u_s[...])
    keep = above | (equal & (rank < (TOPK - n_above).astype(F32)))
    bias = jnp.where(keep, 0.0, -jnp.inf)

    q = q_ref[...]
    n_rep = H_B // N_KV
    bias_g = jnp.concatenate([bias] * n_rep, axis=0)
    for n in range(N_KV):
        kn = k_s[:, n * HD_B:(n + 1) * HD_B]
        vn = v_s[:, n * HD_B:(n + 1) * HD_B]
        qg = jnp.concatenate([q[:, (n * n_rep + g) * HD_B:(n * n_rep + g + 1) * HD_B] for g in range(n_rep)], axis=0)
        s = _dot_nt(qg.astype(MXU_DTYPE), kn) * (HD_B ** -0.5) + bias_g
        m = jnp.max(s, axis=1, keepdims=True)
        p = jnp.exp(s - m)
        den = jnp.sum(p, axis=1, keepdims=True)
        o = _dot(p.astype(MXU_DTYPE), vn) / den
        for g in range(n_rep):
            sl = slice((n * n_rep + g) * HD_B, (n * n_rep + g + 1) * HD_B)
            o_ref[:, sl] = o[g * tq:(g + 1) * tq, :].astype(o_ref.dtype)


def _branch_b(proj, row0, nb, seq, tq, chunked, cache=None, layer=0, qt0=0, n_qt=None):
    nq = seq // tq
    n_qt = nq if n_qt is None else n_qt
    rq0 = row0 // tq
    rk0 = row0 // seq
    lc = 0 if cache is None else cache[0].shape[2]
    lk = (qt0 + n_qt) * tq if chunked else seq
    lpad = -(-(lc + lk) // LANE) * LANE
    hk = N_KV * HD_B

    def qrows(cb):
        return lambda b, t: (rq0 + b * nq + qt0 + t, cb)

    def krows(cb):
        return lambda b, t: (rk0 + b, cb)

    in_specs = [pl.BlockSpec((tq, H_B * HD_B), qrows(C_QB // (H_B * HD_B))),
                pl.BlockSpec((tq, H_I * D_I), qrows(C_QI // (H_I * D_I))),
                pl.BlockSpec((tq, LANE), qrows(C_WI // LANE)),
                pl.BlockSpec((seq, hk), krows(C_KB // hk)),
                pl.BlockSpec((seq, hk), krows(C_VB // hk)),
                pl.BlockSpec((seq, LANE), krows(C_KIA // LANE)),
                pl.BlockSpec((seq, LANE), krows(C_KIB // LANE))]
    args = [proj] * 7
    if cache is not None:
        in_specs += [pl.BlockSpec((None, None, lc, hk), lambda b, t: (layer, b, 0, 0)),
                     pl.BlockSpec((None, None, lc, hk), lambda b, t: (layer, b, 0, 0)),
                     pl.BlockSpec((None, None, lc, LANE), lambda b, t: (layer, b, 0, 0)),
                     pl.BlockSpec((None, None, lc, LANE), lambda b, t: (layer, b, 0, 0))]
        args += list(cache)
    return pl.pallas_call(
        functools.partial(_dsa_body, tq=tq, lc=lc, lk=lk, lpad=lpad, chunked=chunked, qt0=qt0),
        grid=(nb, n_qt),
        in_specs=in_specs,
        out_specs=pl.BlockSpec((tq, H_B * HD_B), lambda b, t: (b * n_qt + t, 0)),
        out_shape=jax.ShapeDtypeStruct((nb * n_qt * tq, H_B * HD_B), BF16),
        scratch_shapes=[pltpu.VMEM((lpad, hk), MXU_DTYPE), pltpu.VMEM((lpad, hk), MXU_DTYPE),
                        pltpu.VMEM((lpad, LANE), MXU_DTYPE), pltpu.VMEM((lpad, LANE), MXU_DTYPE),
                        pltpu.VMEM((tq, lpad), jnp.int32), pltpu.VMEM((lpad, lpad), MXU_DTYPE)],
        compiler_params=_params(("parallel", "arbitrary")),
    )(*args)


def _branch_b_prompt(proj, tq=128, tiles_per_call=2):
    nq = SEQ // tq
    bands = [_branch_b(proj, 0, BATCH, SEQ, tq, True, qt0=q0, n_qt=tiles_per_call).reshape(BATCH, -1, H_B * HD_B)
             for q0 in range(0, nq, tiles_per_call)]
    return jnp.concatenate(bands, axis=1).reshape(T_PROMPT, H_B * HD_B)


W_SEG2_SRC, W_SEG3_SRC = 8272, 14432
W_KI_SRC, W_GL_SRC = 8192, 14416
PROJ_TN = 512


def _proj_body(x_ref, w_ref, gl_ref, o_ref, wt_ref):
    n = pl.program_id(1)
    small_tile = C_KIA // PROJ_TN
    zero_tile = small_tile + 1

    @pl.when((n != small_tile) & (n != zero_tile))
    def _regular():
        o_ref[...] = _dot_nt(x_ref[...], w_ref[0].astype(MXU_DTYPE))

    @pl.when(n == small_tile)
    def _small():
        wt_ref[...] = jnp.zeros_like(wt_ref)
        ki = w_ref[0, 0:D_I, :].astype(MXU_DTYPE)
        wt_ref[0:D_I, :] = ki
        wt_ref[2 * LANE - D_I:2 * LANE, :] = ki
        wt_ref[2 * LANE:2 * LANE + H_I, :] = w_ref[0, D_I:D_I + H_I, :].astype(MXU_DTYPE)
        wt_ref[3 * LANE:3 * LANE + GATE_RANK, :] = gl_ref[0].astype(MXU_DTYPE)
        o_ref[...] = _dot_nt(x_ref[...], wt_ref[...])

    @pl.when(n == zero_tile)
    def _zero():
        o_ref[...] = jnp.zeros_like(o_ref)


def _in_proj(xb, w_t, layer, tm=1024):
    t_dim, d = xb.shape
    tn = PROJ_TN
    small_tile = C_KIA // tn
    seg2_tile0, seg3_tile0 = C_QC // tn, C_GZ // tn

    def w_row(n):
        seg2 = W_SEG2_SRC + (n - seg2_tile0) * tn
        seg3 = W_SEG3_SRC + (n - seg3_tile0) * tn
        aligned = n * tn
        row = jnp.where(n >= seg3_tile0, seg3, jnp.where(n >= small_tile, W_KI_SRC, jnp.where(n >= seg2_tile0, seg2, aligned)))
        return pl.multiple_of(row, 16)

    return pl.pallas_call(
        _proj_body,
        grid=(t_dim // tm, N_PROJ // tn),
        in_specs=[pl.BlockSpec((tm, d), lambda m, n: (m, 0)),
                  pl.BlockSpec((pl.Element(1), pl.Element(tn), pl.Element(d)), lambda m, n: (layer, w_row(n), 0)),
                  pl.BlockSpec((pl.Element(1), pl.Element(GATE_RANK), pl.Element(d)),
                               lambda m, n: (layer, W_GL_SRC, 0))],
        out_specs=pl.BlockSpec((tm, tn), lambda m, n: (m, n)),
        out_shape=jax.ShapeDtypeStruct((t_dim, N_PROJ), F32),
        scratch_shapes=[pltpu.VMEM((tn, d), MXU_DTYPE)],
        compiler_params=_params(("parallel", "arbitrary")),
    )(xb, w_t, w_t)


def kernel(x_prompt, x_sample, cache_k, cache_v, cache_kidx, state_lru, state_conv, state_gla, w_in, conv_w, conv_b, w_rec_gate, b_rec_gate, w_in_gate, b_in_gate, lru_lambda, w_gla_gate, b_gla_gate, gla_norm_g, w_branch, b_branch_gate, w_out, ln1_g, ln1_b, ln2_g, ln2_b, w_ff_gate, w_ff_up, w_ff_down, w_router, w_exp_gate, w_exp_up, w_exp_down):
    hk = N_KV * HD_B
    x = jnp.concatenate([x_prompt.reshape(T_PROMPT, D_MODEL), x_sample.reshape(T_SAMPLE, D_MODEL)], axis=0)
    xb = x.astype(BF16)

    w_t = jnp.swapaxes(w_in, 1, 2)
    w_ffg = _cast_pad(w_ff_gate, D_MODEL, D_FF_PAD)
    w_ffu = _cast_pad(w_ff_up, D_MODEL, D_FF_PAD)
    w_ffd = _cast_pad(w_ff_down, D_FF_PAD, D_MODEL)
    w_router_p = jnp.pad(w_router, ((0, 0), (0, 0), (0, LANE - N_EXPERTS)))
    wgg_p = jnp.pad(w_gla_gate, ((0, 0), (0, LANE - GATE_RANK), (0, 0)))
    wggt_p = jnp.swapaxes(wgg_p, 1, 2)
    cache_k2 = cache_k.reshape(DEPTH, DEC_BATCH, PAST_LEN, hk)
    cache_v2 = cache_v.reshape(DEPTH, DEC_BATCH, PAST_LEN, hk)
    cache_kia = jnp.pad(cache_kidx, ((0, 0), (0, 0), (0, 0), (0, LANE - D_I)))
    cache_kib = jnp.pad(cache_kidx, ((0, 0), (0, 0), (0, 0), (LANE - D_I, 0)))
    zeros_conv = jnp.zeros((BATCH, CONV_W - 1, W_A), F32)
    zeros_lru = jnp.zeros((BATCH, W_A), F32)

    outs_p = [[] for _ in range(6)]
    outs_s = [[] for _ in range(6)]
    for l in range(DEPTH):
        proj = _in_proj(xb, w_t, l)

        lru_args = (conv_w[l], conv_b[l], w_rec_gate[l], b_rec_gate[l], w_in_gate[l], b_in_gate[l], lru_lambda[l])
        a_p, lru_p = _branch_a(proj, 0, BATCH, SEQ, 256, zeros_conv, zeros_lru, *lru_args)
        a_s, lru_s = _branch_a(proj, T_PROMPT, DEC_BATCH, DEC_SEQ, DEC_SEQ, state_conv[l], state_lru[l], *lru_args)

        b_p = _branch_b_prompt(proj)
        b_s = _branch_b(proj, T_PROMPT, DEC_BATCH, DEC_SEQ, DEC_SEQ, False,
                        cache=(cache_k2, cache_v2, cache_kia, cache_kib), layer=l)

        gla_args = (wgg_p[l], wggt_p[l], b_gla_gate[l], gla_norm_g[l])
        c_p, gla_p = _branch_c(proj, 0, BATCH, SEQ, 64, None, l, *gla_args)
        c_s, gla_s = _branch_c(proj, T_PROMPT, DEC_BATCH, DEC_SEQ, DEC_SEQ, state_gla, l, *gla_args)

        branches = jnp.stack([jnp.concatenate([a_p, a_s], axis=0), jnp.concatenate([b_p, b_s], axis=0),
                              jnp.concatenate([c_p, c_s], axis=0)])
        mixed = _merge(branches, w_branch, proj, b_branch_gate[l], l)
        y = _matmul(mixed, w_out, (l,), F32, tm=1024, tn=512, tk=D_MODEL)
        x, xb = _ln_residual(x, y, ln1_g[l], ln1_b[l])

        i = l // 2
        if l % 2 == 0:
            h = _glu_dense(xb, w_ffg, w_ffu, i)
            f = _matmul(h, w_ffd, (i,), F32, tm=1024, tn=1024, tk=D_FF_PAD // 4)
        else:
            combine = _router(x, w_router_p[i])
            f = _moe_routed(xb, combine, w_exp_gate, w_exp_up, w_exp_down, i)
        x, xb = _ln_residual(x, f, ln2_g[l], ln2_b[l])

        pp, ps = proj[:T_PROMPT], proj[T_PROMPT:]
        for dst, rows, nb, seq, lru_h, gla_st in ((outs_p, pp, BATCH, SEQ, lru_p, gla_p),
                                                  (outs_s, ps, DEC_BATCH, DEC_SEQ, lru_s, gla_s)):
            dst[0].append(rows[:, C_KB:C_KB + hk].reshape(nb, seq, N_KV, HD_B))
            dst[1].append(rows[:, C_VB:C_VB + hk].reshape(nb, seq, N_KV, HD_B))
            dst[2].append(rows[:, C_KIA:C_KIA + D_I].reshape(nb, seq, D_I))
            dst[3].append(lru_h)
            dst[4].append(rows[:, C_XA:C_XA + W_A].reshape(nb, seq, W_A)[:, seq - (CONV_W - 1):])
            dst[5].append(gla_st)

    k_p, v_p, ki_p, lru_po, conv_p, gla_po = [jnp.stack(o) for o in outs_p]
    k_s, v_s, ki_s, lru_so, conv_s, gla_so = [jnp.stack(o) for o in outs_s]
    return (x[:T_PROMPT].reshape(BATCH, SEQ, D_MODEL), x[T_PROMPT:].reshape(DEC_BATCH, DEC_SEQ, D_MODEL),
            k_p, v_p, ki_p, lru_po, conv_p, gla_po, k_s, v_s, ki_s, lru_so, conv_s, gla_so)
```

```python
import functools

import jax
import jax.numpy as jnp
from jax import lax
from jax.experimental import pallas as pl
from jax.experimental.pallas import tpu as pltpu

F32 = jnp.float32
BF16 = jnp.bfloat16
MXU_DTYPE = BF16
HIGHEST = lax.Precision.HIGHEST

D_MODEL = 4096
BATCH, SEQ = 4, 2048
DEPTH = 4
DEC_BATCH, DEC_SEQ = 32, 32
PAST_LEN = 1024
T_PROMPT = BATCH * SEQ
T_SAMPLE = DEC_BATCH * DEC_SEQ
T_ALL = T_PROMPT + T_SAMPLE
CHUNK = 64
W_BRANCH = 2048
W_A = W_BRANCH
NB_A = 16
BW_A = W_A // NB_A
CONV_W = 4
LRU_C = 8.0
H_B, HD_B, N_KV = 16, 128, 4
H_I, D_I = 16, 64
TOPK = 256
IDX_W_SCALE = (H_I ** -0.5) * (D_I ** -0.5)
H_C, DK_C, DV_C = 4, 256, 512
GATE_RANK = 16
GATE_NORM = 16.0
N_BRANCH = 3
D_FF = 11008
D_FF_PAD = 11264
N_EXPERTS = 8
D_FF_E = 7168
ALPHA = (2.0 * DEPTH) ** 0.25
LN_EPS = 1e-5
RMS_EPS = 1e-6
LANE = 128
INT_MIN = -2 ** 31

C_XA, C_GA, C_QB, C_KB, C_VB, C_QI = 0, 2048, 4096, 6144, 6656, 7168
C_QC, C_KC, C_VC, C_RC = 8192, 9216, 10240, 12288
C_KIA, C_KIB, C_WI, C_GL = 14336, 14464, 14592, 14720
C_GZ = 15360
N_PROJ = C_GZ + N_BRANCH * D_MODEL

VMEM_LIMIT = 56 * 1024 * 1024


def _params(sem):
    return pltpu.CompilerParams(dimension_semantics=sem, vmem_limit_bytes=VMEM_LIMIT)


def _sigmoid(x):
    return 1.0 / (1.0 + jnp.exp(-x))


def _log_sigmoid(x):
    return jnp.minimum(x, 0.0) - jnp.log(1.0 + jnp.exp(-jnp.abs(x)))


def _dot(a, b):
    return jnp.dot(a, b, preferred_element_type=F32)


def _dot_nt(a, b):
    return lax.dot_general(a, b, (((1,), (1,)), ((), ())), preferred_element_type=F32)


def _dot_tn(a, b):
    return lax.dot_general(a, b, (((0,), (0,)), ((), ())), preferred_element_type=F32)


def _mm_body(x_ref, w_ref, o_ref, acc_ref):
    k = pl.program_id(2)

    @pl.when(k == 0)
    def _init():
        acc_ref[...] = jnp.zeros_like(acc_ref)

    acc_ref[...] += _dot(x_ref[...].astype(MXU_DTYPE), w_ref[...].astype(MXU_DTYPE))

    @pl.when(k == pl.num_programs(2) - 1)
    def _fin():
        o_ref[...] = acc_ref[...].astype(o_ref.dtype)


def _mm_full_k_body(x_ref, w_ref, o_ref):
    o_ref[...] = _dot(x_ref[...].astype(MXU_DTYPE), w_ref[...].astype(MXU_DTYPE)).astype(o_ref.dtype)


def _matmul(x, w, lead, out_dtype, tm, tn, tk):
    m_dim, k_dim = x.shape
    n_dim = w.shape[-1]
    nl = len(lead)
    full_k = tk == k_dim
    return pl.pallas_call(
        _mm_full_k_body if full_k else _mm_body,
        grid=(m_dim // tm, n_dim // tn, k_dim // tk),
        in_specs=[pl.BlockSpec((tm, tk), lambda m, n, k: (m, k)),
                  pl.BlockSpec((None,) * nl + (tk, tn), lambda m, n, k: lead + (k, n))],
        out_specs=pl.BlockSpec((tm, tn), lambda m, n, k: (m, n)),
        out_shape=jax.ShapeDtypeStruct((m_dim, n_dim), out_dtype),
        scratch_shapes=[] if full_k else [pltpu.VMEM((tm, tn), F32)],
        compiler_params=_params(("parallel", "parallel", "arbitrary")),
    )(x, w)


def _cast_pad_body(x_ref, o_ref, *, rows, cols):
    tr, tc = o_ref.shape
    r = pl.program_id(1) * tr + lax.broadcasted_iota(jnp.int32, (tr, tc), 0)
    c = pl.program_id(2) * tc + lax.broadcasted_iota(jnp.int32, (tr, tc), 1)
    o_ref[...] = jnp.where((r < rows) & (c < cols), x_ref[...], 0.0).astype(o_ref.dtype)


def _cast_pad(w, rows_pad, cols_pad, tr=512, tc=1024):
    n_l, rows, cols = w.shape
    assert (rows_pad - rows) < tr and (cols_pad - cols) < tc
    spec = pl.BlockSpec((None, tr, tc), lambda l, i, j: (l, i, j))
    return pl.pallas_call(
        functools.partial(_cast_pad_body, rows=rows, cols=cols),
        grid=(n_l, rows_pad // tr, cols_pad // tc),
        in_specs=[spec], out_specs=spec,
        out_shape=jax.ShapeDtypeStruct((n_l, rows_pad, cols_pad), BF16),
        compiler_params=_params(("parallel", "parallel", "parallel")),
    )(w)


def _ln_body(x_ref, y_ref, g_ref, b_ref, o_ref, ob_ref):
    s = ALPHA * x_ref[...] + y_ref[...]
    mu = jnp.mean(s, axis=-1, keepdims=True)
    d = s - mu
    var = jnp.mean(d * d, axis=-1, keepdims=True)
    o = d * lax.rsqrt(var + LN_EPS) * g_ref[...] + b_ref[...]
    o_ref[...] = o
    ob_ref[...] = o.astype(BF16)


def _ln_residual(x, y, g, b, tm=256):
    t_dim = x.shape[0]
    row = pl.BlockSpec((tm, D_MODEL), lambda m: (m, 0))
    vec = pl.BlockSpec((1, D_MODEL), lambda m: (0, 0))
    return pl.pallas_call(
        _ln_body,
        grid=(t_dim // tm,),
        in_specs=[row, row, vec, vec],
        out_specs=[row, row],
        out_shape=[jax.ShapeDtypeStruct((t_dim, D_MODEL), F32), jax.ShapeDtypeStruct((t_dim, D_MODEL), BF16)],
        compiler_params=_params(("parallel",)),
    )(x, y, g.reshape(1, D_MODEL), b.reshape(1, D_MODEL))


def _glu_body(x_ref, wg_ref, wu_ref, o_ref, *, n_live):
    n = pl.program_id(1)

    @pl.when(n < n_live)
    def _live():
        x = x_ref[...]
        g = _dot(x, wg_ref[...].astype(MXU_DTYPE))
        u = _dot(x, wu_ref[...].astype(MXU_DTYPE))
        o_ref[...] = (g * _sigmoid(g) * u).astype(o_ref.dtype)

    @pl.when(n >= n_live)
    def _pad():
        o_ref[...] = jnp.zeros_like(o_ref)


def _glu_dense(xb, wg, wu, layer, n_pad, tm=1024, tn=256):
    t_dim, k_dim = xb.shape
    n_live = wg.shape[-1] // tn
    assert n_live * tn == wg.shape[-1] and n_pad % tn == 0
    wspec = pl.BlockSpec((None, k_dim, tn), lambda m, n: (layer, 0, jnp.minimum(n, n_live - 1)))
    return pl.pallas_call(
        functools.partial(_glu_body, n_live=n_live),
        grid=(t_dim // tm, n_pad // tn),
        in_specs=[pl.BlockSpec((tm, k_dim), lambda m, n: (m, 0)), wspec, wspec],
        out_specs=pl.BlockSpec((tm, tn), lambda m, n: (m, n)),
        out_shape=jax.ShapeDtypeStruct((t_dim, n_pad), BF16),
        compiler_params=_params(("parallel", "arbitrary")),
    )(xb, wg, wu)


MOE_TM = 1024
MOE_G = 256
MOE_SRC = 512


def _moe_plan(combine, n_exp, tm, g, src):
    i32 = jnp.int32
    t_dim = combine.shape[0]
    n_rows = 2 * t_dim + n_exp * tm
    mask = combine[:, :n_exp] > 0.0
    mi = mask.astype(i32)
    cum = jnp.cumsum(mi, axis=0)
    cnt = cum[-1]
    gs = (cnt + tm - 1) // tm * tm
    g_end = jnp.cumsum(gs)
    g0 = g_end - gs
    rowid = jnp.where(mask, g0[None, :] + cum - mi, -1).astype(i32)
    used_rows = g_end[-1]

    n_mt = n_rows // tm
    mt_used = used_rows // tm
    mt_row0 = jnp.arange(n_mt, dtype=i32) * tm
    mt_exp = jnp.minimum(jnp.searchsorted(g_end, mt_row0, side="right"), n_exp - 1)
    mt_live = jnp.clip(cnt[mt_exp] - (mt_row0 - g0[mt_exp]), 0, tm)

    n_gt = n_rows // g
    n_sb = t_dim // src
    gt_used = used_rows // g
    r0 = jnp.arange(n_gt, dtype=i32) * g
    gt_exp = jnp.minimum(jnp.searchsorted(g_end, r0, side="right"), n_exp - 1).astype(i32)
    rank0 = r0 - g0[gt_exp]
    rank1 = jnp.minimum(rank0 + g, cnt[gt_exp]) - 1
    cb = cum[src - 1::src].T
    cb_t = cb[gt_exp]
    fb = jnp.sum(cb_t <= rank0[:, None], axis=1)
    lb = jnp.sum(cb_t <= rank1[:, None], axis=1)
    has_rows = rank1 >= rank0
    fb = jnp.where(has_rows, jnp.minimum(fb, n_sb - 1), 0)
    lb = jnp.where(has_rows, jnp.minimum(lb, n_sb - 1), 0)
    nblk = jnp.where(jnp.arange(n_gt) < gt_used, lb - fb + 1, 0)
    off_end = jnp.cumsum(nblk)
    off = off_end - nblk
    n_items = n_gt + n_exp * n_sb
    w = jnp.arange(n_items, dtype=i32)
    it_valid = w < off_end[-1]
    it_tile = jnp.minimum(jnp.searchsorted(off_end, w, side="right"), jnp.maximum(gt_used - 1, 0)).astype(i32)
    it_blk = jnp.clip(fb[it_tile] + w - off[it_tile], 0, lb[it_tile]).astype(i32)
    it_first = (w == off[it_tile])
    gather = jnp.stack([it_tile, it_blk, gt_exp[it_tile], it_first.astype(i32), it_valid.astype(i32)])

    ce = cum[g - 1::g]
    cs = jnp.concatenate([jnp.zeros((1, n_exp), i32), ce[:-1]], axis=0)
    a = g0[None, :] + cs
    b = g0[None, :] + ce
    blk0 = jnp.clip(a // g, 0, n_gt - 1)
    blk1 = jnp.clip((b - 1) // g, 0, n_gt - 1)
    v0 = b > a
    v1 = v0 & (blk1 > blk0)
    c_blk = jnp.stack([blk0, jnp.where(v1, blk1, blk0)], axis=-1).reshape(-1).astype(i32)
    c_val = jnp.stack([v0, v1], axis=-1).reshape(-1).astype(i32)
    return dict(n_rows=n_rows, rowid=rowid, mt_used=mt_used.reshape(1).astype(i32), mt_exp=mt_exp.astype(i32),
                mt_live=mt_live.astype(i32), gather=gather, c_blk=c_blk, c_val=c_val)


def _moe_gather_body(it_ref, x_ref, rid_ref, o_ref, *, g):
    w = pl.program_id(0)
    tile, e, first, valid = it_ref[0, w], it_ref[2, w], it_ref[3, w], it_ref[4, w]

    @pl.when(valid == 1)
    def _():
        rid = rid_ref[pl.ds(e, 1), :]
        rows = tile * g + lax.broadcasted_iota(jnp.int32, (g, rid.shape[1]), 0)
        part = _dot((rid == rows).astype(MXU_DTYPE), x_ref[...]).astype(o_ref.dtype)

        @pl.when(first == 1)
        def _set():
            o_ref[...] = part

        @pl.when(first == 0)
        def _add():
            o_ref[...] += part


def _moe_gather(xb, rowid_t, plan, g, src):
    d = xb.shape[1]
    items = plan["gather"]
    return pl.pallas_call(
        functools.partial(_moe_gather_body, g=g),
        grid_spec=pltpu.PrefetchScalarGridSpec(
            num_scalar_prefetch=1, grid=(items.shape[1],),
            in_specs=[pl.BlockSpec((src, d), lambda w, it: (it[1, w], 0)),
                      pl.BlockSpec((rowid_t.shape[0], src), lambda w, it: (0, it[1, w]))],
            out_specs=pl.BlockSpec((g, d), lambda w, it: (it[0, w], 0))),
        out_shape=jax.ShapeDtypeStruct((plan["n_rows"], d), xb.dtype),
        compiler_params=_params(("arbitrary",)),
    )(items, xb, rowid_t)


def _moe_up_body(used_ref, exp_ref, live_ref, x_ref, wg_ref, wu_ref, o_ref):
    i = pl.program_id(0)
    half = x_ref.shape[0] // 2

    def run(rows):
        x = x_ref[0:rows, :]
        gate = _dot(x, wg_ref[...].astype(MXU_DTYPE))
        up = _dot(x, wu_ref[...].astype(MXU_DTYPE))
        o_ref[0:rows, :] = (gate * _sigmoid(gate) * up).astype(o_ref.dtype)

    @pl.when((i < used_ref[0]) & (live_ref[i] > half))
    def _full():
        run(2 * half)

    @pl.when((i < used_ref[0]) & (live_ref[i] <= half))
    def _half():
        run(half)


def _grouped_index(nn, nk):
    def pick(i, n, k, used):
        live = i < used[0]
        last = jnp.maximum(used[0] - 1, 0)
        return jnp.where(live, i, last), jnp.where(live, n, nn - 1), jnp.where(live, k, nk - 1)
    return pick


def _moe_up(xg, wg, wu, layer, plan, tm, tn):
    n_rows, d = xg.shape
    f = wg.shape[-1]
    nn = f // tn
    pick = _grouped_index(nn, 1)

    def x_map(i, n, used, exp, live):
        return pick(i, n, 0, used)[0], 0

    def w_map(i, n, used, exp, live):
        ii, n2, _ = pick(i, n, 0, used)
        return layer, exp[ii], 0, n2

    def o_map(i, n, used, exp, live):
        return pick(i, n, 0, used)[:2]

    wspec = pl.BlockSpec((None, None, d, tn), w_map)
    return pl.pallas_call(
        _moe_up_body,
        grid_spec=pltpu.PrefetchScalarGridSpec(
            num_scalar_prefetch=3, grid=(n_rows // tm, nn),
            in_specs=[pl.BlockSpec((tm, d), x_map), wspec, wspec],
            out_specs=pl.BlockSpec((tm, tn), o_map)),
        out_shape=jax.ShapeDtypeStruct((n_rows, f), xg.dtype),
        compiler_params=_params(("arbitrary", "arbitrary")),
    )(plan["mt_used"], plan["mt_exp"], plan["mt_live"], xg, wg, wu)


def _moe_down_body(used_ref, exp_ref, live_ref, x_ref, w_ref, hi_ref, lo_ref):
    i = pl.program_id(0)
    half = x_ref.shape[0] // 2

    def run(rows):
        y = _dot(x_ref[0:rows, :], w_ref[...].astype(MXU_DTYPE))
        hi = y.astype(hi_ref.dtype)
        hi_ref[0:rows, :] = hi
        lo_ref[0:rows, :] = (y - hi.astype(F32)).astype(lo_ref.dtype)

    @pl.when((i < used_ref[0]) & (live_ref[i] > half))
    def _full():
        run(2 * half)

    @pl.when((i < used_ref[0]) & (live_ref[i] <= half))
    def _half():
        run(half)


def _moe_down(h, wd, layer, plan, tm, tn):
    n_rows, f = h.shape
    d = wd.shape[-1]
    nn = d // tn
    pick = _grouped_index(nn, 1)

    def x_map(i, n, used, exp, live):
        return pick(i, n, 0, used)[0], 0

    def w_map(i, n, used, exp, live):
        ii, n2, _ = pick(i, n, 0, used)
        return layer, exp[ii], 0, n2

    def o_map(i, n, used, exp, live):
        return pick(i, n, 0, used)[:2]

    piece = jax.ShapeDtypeStruct((n_rows, d), h.dtype)
    return pl.pallas_call(
        _moe_down_body,
        grid_spec=pltpu.PrefetchScalarGridSpec(
            num_scalar_prefetch=3, grid=(n_rows // tm, nn),
            in_specs=[pl.BlockSpec((tm, f), x_map), pl.BlockSpec((None, None, f, tn), w_map)],
            out_specs=[pl.BlockSpec((tm, tn), o_map), pl.BlockSpec((tm, tn), o_map)]),
        out_shape=[piece, piece],
        compiler_params=_params(("arbitrary", "arbitrary")),
    )(plan["mt_used"], plan["mt_exp"], plan["mt_live"], h, wd)


def _moe_combine_body(blk_ref, val_ref, hi_ref, lo_ref, rid_ref, c_ref, o_ref, *, g, n_exp):
    m, e, s = pl.program_id(0), pl.program_id(1), pl.program_id(2)
    item = (m * n_exp + e) * 2 + s

    @pl.when((e == 0) & (s == 0))
    def _init():
        o_ref[...] = jnp.zeros_like(o_ref)

    @pl.when(val_ref[item] == 1)
    def _():
        rid = rid_ref[...]
        rid_e = jnp.sum(jnp.where(lax.broadcasted_iota(jnp.int32, rid.shape, 1) == e, rid, 0), axis=1, keepdims=True)
        c = c_ref[...]
        c_e = jnp.sum(jnp.where(lax.broadcasted_iota(jnp.int32, c.shape, 1) == e, c, 0.0), axis=1, keepdims=True)
        rows = blk_ref[item] * g + lax.broadcasted_iota(jnp.int32, (g, g), 1)
        onehot = (rid_e == rows).astype(MXU_DTYPE)
        o_ref[...] += c_e * (_dot(onehot, hi_ref[...]) + _dot(onehot, lo_ref[...]))


def _moe_combine(y_hi, y_lo, rowid, combine, plan, g):
    t_dim, n_exp = rowid.shape
    d = y_hi.shape[1]

    def y_map(m, e, s, blk, val):
        return blk[(m * n_exp + e) * 2 + s], 0

    return pl.pallas_call(
        functools.partial(_moe_combine_body, g=g, n_exp=n_exp),
        grid_spec=pltpu.PrefetchScalarGridSpec(
            num_scalar_prefetch=2, grid=(t_dim // g, n_exp, 2),
            in_specs=[pl.BlockSpec((g, d), y_map), pl.BlockSpec((g, d), y_map),
                      pl.BlockSpec((g, n_exp), lambda m, e, s, blk, val: (m, 0)),
                      pl.BlockSpec((g, combine.shape[1]), lambda m, e, s, blk, val: (m, 0))],
            out_specs=pl.BlockSpec((g, d), lambda m, e, s, blk, val: (m, 0))),
        out_shape=jax.ShapeDtypeStruct((t_dim, d), F32),
        compiler_params=_params(("arbitrary", "arbitrary", "arbitrary")),
    )(plan["c_blk"], plan["c_val"], y_hi, y_lo, rowid, combine)


def _moe_routed(xb, combine, wg, wu, wd, layer, tm=MOE_TM, g=MOE_G, src=MOE_SRC, tn_up=256, tn_down=256):
    n_exp = wg.shape[1]
    plan = _moe_plan(combine, n_exp, tm, g, src)
    xg = _moe_gather(xb, plan["rowid"].T, plan, g, src)
    h = _moe_up(xg, wg, wu, layer, plan, tm, tn_up)
    y_hi, y_lo = _moe_down(h, wd, layer, plan, tm, tn_down)
    return _moe_combine(y_hi, y_lo, plan["rowid"], combine, plan, g)


def _router_body(x_ref, w_ref, c_ref):
    logits = jnp.dot(x_ref[...], w_ref[...], preferred_element_type=F32, precision=HIGHEST)
    lane = lax.broadcasted_iota(jnp.int32, logits.shape, 1)
    logits = jnp.where(lane < N_EXPERTS, logits, -jnp.inf)
    m1 = jnp.max(logits, axis=1, keepdims=True)
    i1 = jnp.min(jnp.where(logits == m1, lane, LANE), axis=1, keepdims=True)
    rest = jnp.where(lane == i1, -jnp.inf, logits)
    m2 = jnp.max(rest, axis=1, keepdims=True)
    i2 = jnp.min(jnp.where(rest == m2, lane, LANE), axis=1, keepdims=True)
    e2 = jnp.exp(m2 - m1)
    w1 = 1.0 / (1.0 + e2)
    w2 = e2 / (1.0 + e2)
    c_ref[...] = jnp.where(lane == i1, w1, 0.0) + jnp.where(lane == i2, w2, 0.0)


def _router(x, w_pad, tm=512):
    t_dim = x.shape[0]
    return pl.pallas_call(
        _router_body,
        grid=(t_dim // tm,),
        in_specs=[pl.BlockSpec((tm, D_MODEL), lambda m: (m, 0)),
                  pl.BlockSpec((D_MODEL, LANE), lambda m: (0, 0))],
        out_specs=pl.BlockSpec((tm, LANE), lambda m: (m, 0)),
        out_shape=jax.ShapeDtypeStruct((t_dim, LANE), F32),
        compiler_params=_params(("parallel",)),
    )(x, w_pad)


def _merge_body(br_ref, w_ref, gz_ref, bg_ref, o_ref, acc_ref):
    j = pl.program_id(2)

    @pl.when(j == 0)
    def _init():
        acc_ref[...] = jnp.zeros_like(acc_ref)

    pj = _dot(br_ref[...].astype(MXU_DTYPE), w_ref[...].astype(MXU_DTYPE))
    acc_ref[...] += _sigmoid(gz_ref[...] + bg_ref[...]) * pj

    @pl.when(j == N_BRANCH - 1)
    def _fin():
        o_ref[...] = acc_ref[...].astype(o_ref.dtype)


def _merge(branches, w_branch, proj, b_gate, layer, tm=1024, tn=1024):
    nt = D_MODEL // tn
    gz_blk = C_GZ // tn
    return pl.pallas_call(
        _merge_body,
        grid=(T_ALL // tm, nt, N_BRANCH),
        in_specs=[pl.BlockSpec((None, tm, W_BRANCH), lambda m, n, j: (j, m, 0)),
                  pl.BlockSpec((None, None, W_BRANCH, tn), lambda m, n, j: (layer, j, 0, n)),
                  pl.BlockSpec((tm, tn), lambda m, n, j: (m, gz_blk + j * nt + n)),
                  pl.BlockSpec((1, tn), lambda m, n, j: (0, j * nt + n))],
        out_specs=pl.BlockSpec((tm, tn), lambda m, n, j: (m, n)),
        out_shape=jax.ShapeDtypeStruct((T_ALL, D_MODEL), BF16),
        scratch_shapes=[pltpu.VMEM((tm, tn), F32)],
        compiler_params=_params(("parallel", "parallel", "arbitrary")),
    )(branches, w_branch, proj, b_gate.reshape(1, N_BRANCH * D_MODEL))


def _lru_body(xa_ref, ga_ref, buf_ref, h0_ref, cw_ref, cb_ref, wrg_ref, brg_ref, wig_ref, big_ref, lam_ref,
              o_ref, hlast_ref, xp_ref, a_ref, u_ref, h_ref, *, tl):
    t = pl.program_id(1)

    @pl.when(t == 0)
    def _init():
        xp_ref[5:8, :] = buf_ref[...]
        h_ref[...] = h0_ref[...]

    xp_ref[8:8 + tl, :] = xa_ref[...]
    xc = cb_ref[...] + xp_ref[8:8 + tl, :] * cw_ref[3:4, :]
    for j in range(CONV_W - 1):
        xc = xc + xp_ref[5 + j:5 + j + tl, :] * cw_ref[j:j + 1, :]
    xp_ref[5:8, :] = xa_ref[tl - 3:tl, :]

    xcb = xc.astype(MXU_DTYPE)
    for n in range(NB_A):
        sl = slice(n * BW_A, (n + 1) * BW_A)
        xs = xcb[:, sl]
        r = _sigmoid(_dot(xs, wrg_ref[n].astype(MXU_DTYPE)) + brg_ref[:, sl])
        i = _sigmoid(_dot(xs, wig_ref[n].astype(MXU_DTYPE)) + big_ref[:, sl])
        lam = lam_ref[:, sl]
        softplus = jnp.maximum(-lam, 0.0) + jnp.log(1.0 + jnp.exp(-jnp.abs(lam)))
        log_a = -LRU_C * r * softplus
        a_ref[:, sl] = jnp.exp(log_a)
        u_ref[:, sl] = jnp.sqrt(1.0 - jnp.exp(2.0 * log_a)) * (i * xc[:, sl])

    def step(s, h):
        h = a_ref[pl.ds(s, 1), :] * h + u_ref[pl.ds(s, 1), :]
        a_ref[pl.ds(s, 1), :] = h
        return h

    h = lax.fori_loop(0, tl, step, h_ref[...], unroll=8)
    h_ref[...] = h
    hlast_ref[...] = h
    g = ga_ref[...]
    gelu = 0.5 * g * (1.0 + jnp.tanh(0.7978845608028654 * (g + 0.044715 * (g * g * g))))
    o_ref[...] = (a_ref[...] * gelu).astype(o_ref.dtype)


def _branch_a(proj, row0, nb, seq, tl, conv_buf, h0, cw, cb, wrg, brg, wig, big, lam):
    nt = seq // tl
    rb0 = row0 // tl
    vec = pl.BlockSpec((1, W_A), lambda b, t: (0, 0))
    blk = pl.BlockSpec((NB_A, BW_A, BW_A), lambda b, t: (0, 0, 0))
    out, h_last = pl.pallas_call(
        functools.partial(_lru_body, tl=tl),
        grid=(nb, nt),
        in_specs=[pl.BlockSpec((tl, W_A), lambda b, t: (rb0 + b * nt + t, C_XA // W_A)),
                  pl.BlockSpec((tl, W_A), lambda b, t: (rb0 + b * nt + t, C_GA // W_A)),
                  pl.BlockSpec((None, CONV_W - 1, W_A), lambda b, t: (b, 0, 0)),
                  pl.BlockSpec((None, 1, W_A), lambda b, t: (b, 0, 0)),
                  pl.BlockSpec((CONV_W, W_A), lambda b, t: (0, 0)),
                  vec, blk, vec, blk, vec, vec],
        out_specs=[pl.BlockSpec((tl, W_A), lambda b, t: (b * nt + t, 0)),
                   pl.BlockSpec((None, 1, W_A), lambda b, t: (b, 0, 0))],
        out_shape=[jax.ShapeDtypeStruct((nb * seq, W_A), BF16), jax.ShapeDtypeStruct((nb, 1, W_A), F32)],
        scratch_shapes=[pltpu.VMEM((tl + 8, W_A), F32), pltpu.VMEM((tl, W_A), F32), pltpu.VMEM((tl, W_A), F32),
                        pltpu.VMEM((1, W_A), F32)],
        compiler_params=_params(("parallel", "arbitrary")),
    )(proj, proj, conv_buf, h0.reshape(nb, 1, W_A), cw, cb.reshape(1, W_A), wrg, brg.reshape(1, W_A),
      wig, big.reshape(1, W_A), lam.reshape(1, W_A))
    return out, h_last.reshape(nb, W_A)


def _gla_body(*refs, ck, has_s0):
    if has_s0:
        q_ref, k_ref, v_ref, rc_ref, gl_ref, wgg_ref, wggt_ref, bgr_ref, bgc_ref, g_ref, s0_ref, o_ref, s_ref = refs
    else:
        q_ref, k_ref, v_ref, rc_ref, gl_ref, wgg_ref, wggt_ref, bgr_ref, bgc_ref, g_ref, o_ref, s_ref = refs
    c = pl.program_id(1)

    @pl.when(c == 0)
    def _init():
        s_ref[...] = s0_ref[...] if has_s0 else jnp.zeros_like(s_ref)

    gl = gl_ref[...].astype(MXU_DTYPE)
    lg = _log_sigmoid(_dot(gl, wgg_ref[...].astype(MXU_DTYPE)) + bgr_ref[...]) / GATE_NORM
    lg_t = _log_sigmoid(_dot_nt(wggt_ref[...].astype(MXU_DTYPE), gl) + bgc_ref[...]) / GATE_NORM
    row = lax.broadcasted_iota(jnp.int32, (ck, ck), 0)
    col = lax.broadcasted_iota(jnp.int32, (ck, ck), 1)
    tri = row >= col
    bcum_all = jnp.dot(tri.astype(F32), lg, preferred_element_type=F32, precision=HIGHEST)
    b_last_col_all = jnp.sum(lg_t, axis=1, keepdims=True)

    for h in range(H_C):
        ks = slice(h * DK_C, (h + 1) * DK_C)
        vs = slice(h * DV_C, (h + 1) * DV_C)
        bcum = bcum_all[:, ks]
        b_last = bcum[ck - 1:ck, :]
        k = k_ref[:, ks]
        qe = (q_ref[:, ks] * (DK_C ** -0.5) * jnp.exp(bcum)).astype(MXU_DTYPE)
        ke = (k * jnp.exp(-bcum)).astype(MXU_DTYPE)
        kt = (k * jnp.exp(b_last - bcum)).astype(MXU_DTYPE)
        vb = v_ref[:, vs].astype(MXU_DTYPE)
        att = jnp.where(tri, _dot_nt(qe, ke), 0.0)
        s = s_ref[h]
        o = _dot(att.astype(MXU_DTYPE), vb) + _dot(qe, s.astype(MXU_DTYPE))
        s_ref[h] = jnp.exp(b_last_col_all[ks, :]) * s + _dot_tn(kt, vb)

        o = o * lax.rsqrt(jnp.mean(o * o, axis=-1, keepdims=True) + RMS_EPS) * g_ref[...]
        rc = rc_ref[:, vs]
        o_ref[:, vs] = (o * (rc * _sigmoid(rc))).astype(o_ref.dtype)


def _branch_c(proj, row0, nb, seq, ck, s0, layer, wgg_pad, wggt_pad, bgg, gla_g):
    nc = seq // ck
    rb0 = row0 // ck
    hk, hv = H_C * DK_C, H_C * DV_C

    def rows(cb):
        return lambda b, c: (rb0 + b * nc + c, cb)

    def const(b, c):
        return (0, 0)

    state = pl.BlockSpec((None, H_C, DK_C, DV_C), lambda b, c: (b, 0, 0, 0))
    has_s0 = s0 is not None
    s0_spec = [pl.BlockSpec((None, None, H_C, DK_C, DV_C), lambda b, c: (layer, b, 0, 0, 0))] if has_s0 else []
    s0_arg = [s0] if has_s0 else []
    out, s_fin = pl.pallas_call(
        functools.partial(_gla_body, ck=ck, has_s0=has_s0),
        grid=(nb, nc),
        in_specs=[pl.BlockSpec((ck, hk), rows(C_QC // hk)),
                  pl.BlockSpec((ck, hk), rows(C_KC // hk)),
                  pl.BlockSpec((ck, hv), rows(C_VC // hv)),
                  pl.BlockSpec((ck, hv), rows(C_RC // hv)),
                  pl.BlockSpec((ck, LANE), rows(C_GL // LANE)),
                  pl.BlockSpec((LANE, hk), const),
                  pl.BlockSpec((hk, LANE), const),
                  pl.BlockSpec((1, hk), const),
                  pl.BlockSpec((hk, 1), const),
                  pl.BlockSpec((1, DV_C), const)] + s0_spec,
        out_specs=[pl.BlockSpec((ck, hv), lambda b, c: (b * nc + c, 0)), state],
        out_shape=[jax.ShapeDtypeStruct((nb * seq, hv), BF16),
                   jax.ShapeDtypeStruct((nb, H_C, DK_C, DV_C), F32)],
        compiler_params=_params(("parallel", "arbitrary")),
    )(proj, proj, proj, proj, proj, wgg_pad, wggt_pad, bgg.reshape(1, hk), bgg.reshape(hk, 1),
      gla_g.reshape(1, DV_C), *s0_arg)
    return out, s_fin


def _dsa_body(*refs, tq, lc, lk, lpad, chunked, qt0):
    if lc:
        (q_ref, qi_ref, wi_ref, kn_ref, vn_ref, kan_ref, kbn_ref, kc_ref, vc_ref, kac_ref, kbc_ref,
         o_ref, k_s, v_s, ka_s, kb_s, key_s, u_s) = refs
    else:
        (q_ref, qi_ref, wi_ref, kn_ref, vn_ref, kan_ref, kbn_ref,
         o_ref, k_s, v_s, ka_s, kb_s, key_s, u_s) = refs
    qt = qt0 + pl.program_id(1)
    n_keys = lc + lk

    @pl.when(pl.program_id(1) == 0)
    def _stage_keys():
        if lc:
            k_s[0:lc, :] = kc_ref[...].astype(MXU_DTYPE)
            v_s[0:lc, :] = vc_ref[...].astype(MXU_DTYPE)
            ka_s[0:lc, :] = kac_ref[...].astype(MXU_DTYPE)
            kb_s[0:lc, :] = kbc_ref[...].astype(MXU_DTYPE)
        k_s[lc:n_keys, :] = kn_ref[0:lk, :].astype(MXU_DTYPE)
        v_s[lc:n_keys, :] = vn_ref[0:lk, :].astype(MXU_DTYPE)
        ka_s[lc:n_keys, :] = kan_ref[0:lk, :].astype(MXU_DTYPE)
        kb_s[lc:n_keys, :] = kbn_ref[0:lk, :].astype(MXU_DTYPE)
        if lpad > n_keys:
            k_s[n_keys:lpad, :] = jnp.zeros((lpad - n_keys, N_KV * HD_B), MXU_DTYPE)
            v_s[n_keys:lpad, :] = jnp.zeros((lpad - n_keys, N_KV * HD_B), MXU_DTYPE)
            ka_s[n_keys:lpad, :] = jnp.zeros((lpad - n_keys, LANE), MXU_DTYPE)
            kb_s[n_keys:lpad, :] = jnp.zeros((lpad - n_keys, LANE), MXU_DTYPE)
        for r0 in range(0, lpad, LANE):
            rr = r0 + lax.broadcasted_iota(jnp.int32, (LANE, lpad), 0)
            cc = lax.broadcasted_iota(jnp.int32, (LANE, lpad), 1)
            u_s[r0:r0 + LANE, :] = (rr < cc).astype(MXU_DTYPE)

    qi = qi_ref[...].astype(MXU_DTYPE)
    wi = wi_ref[...] * IDX_W_SCALE
    ka = ka_s[...]
    kb = kb_s[...]
    score = jnp.zeros((tq, lpad), F32)
    for p in range(H_I // 2):
        qp = qi[:, p * LANE:(p + 1) * LANE]
        score = score + wi[:, 2 * p:2 * p + 1] * jnp.maximum(_dot_nt(qp, ka), 0.0)
        score = score + wi[:, 2 * p + 1:2 * p + 2] * jnp.maximum(_dot_nt(qp, kb), 0.0)

    score = jnp.where(score == 0.0, 0.0, score)
    bits = lax.bitcast_convert_type(score, jnp.int32)
    key = bits ^ ((bits >> 31) & 0x7FFFFFFF)
    col = lax.broadcasted_iota(jnp.int32, (tq, lpad), 1)
    if chunked:
        pos = qt * tq + lax.broadcasted_iota(jnp.int32, (tq, lpad), 0)
        valid = col < (pos // CHUNK + 1) * CHUNK
    else:
        valid = col < n_keys
    key_s[...] = jnp.where(valid, key, INT_MIN)

    n_grp = 2 if tq >= LANE else 1
    rows_g = tq // n_grp

    def count_ge(grp, cand):
        keys_g = key_s[grp * rows_g:(grp + 1) * rows_g, :]
        return jnp.sum((keys_g >= cand).astype(jnp.int32), axis=1, keepdims=True)

    zero = jnp.zeros((rows_g, 1), jnp.int32)
    prefixes = tuple(jnp.where(count_ge(grp, zero) >= TOPK, 0, INT_MIN).astype(jnp.int32) for grp in range(n_grp))

    def search(i, prefixes):
        bit = jnp.left_shift(jnp.int32(1), 30 - i)
        cands = [p | bit for p in prefixes]
        return tuple(jnp.where(count_ge(grp, c) >= TOPK, c, p) for grp, (c, p) in enumerate(zip(cands, prefixes)))

    prefix = jnp.concatenate(lax.fori_loop(0, 31, search, prefixes), axis=0)

    keys = key_s[...]
    above = keys > prefix
    equal = (keys == prefix) & valid
    n_above = jnp.sum(above.astype(jnp.int32), axis=1, keepdims=True)
    rank = _dot(equal.astype(MXU_DTYPE), u_s[...])
    keep = above | (equal & (rank < (TOPK - n_above).astype(F32)))
    bias = jnp.where(keep, 0.0, -jnp.inf)

    q = q_ref[...]
    n_rep = H_B // N_KV
    bias_g = jnp.concatenate([bias] * n_rep, axis=0)
    for n in range(N_KV):
        kn = k_s[:, n * HD_B:(n + 1) * HD_B]
        vn = v_s[:, n * HD_B:(n + 1) * HD_B]
        qg = jnp.concatenate([q[:, (n * n_rep + g) * HD_B:(n * n_rep + g + 1) * HD_B] for g in range(n_rep)], axis=0)
        s = _dot_nt(qg.astype(MXU_DTYPE), kn) * (HD_B ** -0.5) + bias_g
        m = jnp.max(s, axis=1, keepdims=True)
        p = jnp.exp(s - m)
        den = jnp.sum(p, axis=1, keepdims=True)
        o = _dot(p.astype(MXU_DTYPE), vn) / den
        for g in range(n_rep):
            sl = slice((n * n_rep + g) * HD_B, (n * n_rep + g + 1) * HD_B)
            o_ref[:, sl] = o[g * tq:(g + 1) * tq, :].astype(o_ref.dtype)


def _branch_b(proj, row0, nb, seq, tq, chunked, cache=None, layer=0, qt0=0, n_qt=None):
    nq = seq // tq
    n_qt = nq if n_qt is None else n_qt
    rq0 = row0 // tq
    rk0 = row0 // seq
    lc = 0 if cache is None else cache[0].shape[2]
    lk = (qt0 + n_qt) * tq if chunked else seq
    lpad = -(-(lc + lk) // LANE) * LANE
    hk = N_KV * HD_B

    def qrows(cb):
        return lambda b, t: (rq0 + b * nq + qt0 + t, cb)

    def krows(cb):
        return lambda b, t: (rk0 + b, cb)

    in_specs = [pl.BlockSpec((tq, H_B * HD_B), qrows(C_QB // (H_B * HD_B))),
                pl.BlockSpec((tq, H_I * D_I), qrows(C_QI // (H_I * D_I))),
                pl.BlockSpec((tq, LANE), qrows(C_WI // LANE)),
                pl.BlockSpec((seq, hk), krows(C_KB // hk)),
                pl.BlockSpec((seq, hk), krows(C_VB // hk)),
                pl.BlockSpec((seq, LANE), krows(C_KIA // LANE)),
                pl.BlockSpec((seq, LANE), krows(C_KIB // LANE))]
    args = [proj] * 7
    if cache is not None:
        in_specs += [pl.BlockSpec((None, None, lc, hk), lambda b, t: (layer, b, 0, 0)),
                     pl.BlockSpec((None, None, lc, hk), lambda b, t: (layer, b, 0, 0)),
                     pl.BlockSpec((None, None, lc, LANE), lambda b, t: (layer, b, 0, 0)),
                     pl.BlockSpec((None, None, lc, LANE), lambda b, t: (layer, b, 0, 0))]
        args += list(cache)
    return pl.pallas_call(
        functools.partial(_dsa_body, tq=tq, lc=lc, lk=lk, lpad=lpad, chunked=chunked, qt0=qt0),
        grid=(nb, n_qt),
        in_specs=in_specs,
        out_specs=pl.BlockSpec((tq, H_B * HD_B), lambda b, t: (b * n_qt + t, 0)),
        out_shape=jax.ShapeDtypeStruct((nb * n_qt * tq, H_B * HD_B), BF16),
        scratch_shapes=[pltpu.VMEM((lpad, hk), MXU_DTYPE), pltpu.VMEM((lpad, hk), MXU_DTYPE),
                        pltpu.VMEM((lpad, LANE), MXU_DTYPE), pltpu.VMEM((lpad, LANE), MXU_DTYPE),
                        pltpu.VMEM((tq, lpad), jnp.int32), pltpu.VMEM((lpad, lpad), MXU_DTYPE)],
        compiler_params=_params(("parallel", "arbitrary")),
    )(*args)


def _branch_b_prompt(proj, tq=128, tiles_per_call=2):
    nq = SEQ // tq
    bands = [_branch_b(proj, 0, BATCH, SEQ, tq, True, qt0=q0, n_qt=tiles_per_call).reshape(BATCH, -1, H_B * HD_B)
             for q0 in range(0, nq, tiles_per_call)]
    return jnp.concatenate(bands, axis=1).reshape(T_PROMPT, H_B * HD_B)


W_SEG2_SRC, W_SEG3_SRC = 8272, 14432
W_KI_SRC, W_GL_SRC = 8192, 14416
PROJ_TN = 512


def _proj_body(x_ref, w_ref, gl_ref, o_ref, wt_ref):
    n = pl.program_id(1)
    small_tile = C_KIA // PROJ_TN
    zero_tile = small_tile + 1

    @pl.when((n != small_tile) & (n != zero_tile))
    def _regular():
        o_ref[...] = _dot_nt(x_ref[...], w_ref[0].astype(MXU_DTYPE))

    @pl.when(n == small_tile)
    def _small():
        wt_ref[...] = jnp.zeros_like(wt_ref)
        ki = w_ref[0, 0:D_I, :].astype(MXU_DTYPE)
        wt_ref[0:D_I, :] = ki
        wt_ref[2 * LANE - D_I:2 * LANE, :] = ki
        wt_ref[2 * LANE:2 * LANE + H_I, :] = w_ref[0, D_I:D_I + H_I, :].astype(MXU_DTYPE)
        wt_ref[3 * LANE:3 * LANE + GATE_RANK, :] = gl_ref[0].astype(MXU_DTYPE)
        o_ref[...] = _dot_nt(x_ref[...], wt_ref[...])

    @pl.when(n == zero_tile)
    def _zero():
        o_ref[...] = jnp.zeros_like(o_ref)


def _in_proj(xb, w_t, layer, tm=1024):
    t_dim, d = xb.shape
    tn = PROJ_TN
    small_tile = C_KIA // tn
    seg2_tile0, seg3_tile0 = C_QC // tn, C_GZ // tn

    def w_row(n):
        seg2 = W_SEG2_SRC + (n - seg2_tile0) * tn
        seg3 = W_SEG3_SRC + (n - seg3_tile0) * tn
        aligned = n * tn
        row = jnp.where(n >= seg3_tile0, seg3, jnp.where(n >= small_tile, W_KI_SRC, jnp.where(n >= seg2_tile0, seg2, aligned)))
        return pl.multiple_of(row, 16)

    return pl.pallas_call(
        _proj_body,
        grid=(t_dim // tm, N_PROJ // tn),
        in_specs=[pl.BlockSpec((tm, d), lambda m, n: (m, 0)),
                  pl.BlockSpec((pl.Element(1), pl.Element(tn), pl.Element(d)), lambda m, n: (layer, w_row(n), 0)),
                  pl.BlockSpec((pl.Element(1), pl.Element(GATE_RANK), pl.Element(d)),
                               lambda m, n: (layer, W_GL_SRC, 0))],
        out_specs=pl.BlockSpec((tm, tn), lambda m, n: (m, n)),
        out_shape=jax.ShapeDtypeStruct((t_dim, N_PROJ), F32),
        scratch_shapes=[pltpu.VMEM((tn, d), MXU_DTYPE)],
        compiler_params=_params(("parallel", "arbitrary")),
    )(xb, w_t, w_t)


def kernel(x_prompt, x_sample, cache_k, cache_v, cache_kidx, state_lru, state_conv, state_gla, w_in, conv_w, conv_b, w_rec_gate, b_rec_gate, w_in_gate, b_in_gate, lru_lambda, w_gla_gate, b_gla_gate, gla_norm_g, w_branch, b_branch_gate, w_out, ln1_g, ln1_b, ln2_g, ln2_b, w_ff_gate, w_ff_up, w_ff_down, w_router, w_exp_gate, w_exp_up, w_exp_down):
    hk = N_KV * HD_B
    x = jnp.concatenate([x_prompt.reshape(T_PROMPT, D_MODEL), x_sample.reshape(T_SAMPLE, D_MODEL)], axis=0)
    xb = x.astype(BF16)

    w_t = jnp.swapaxes(w_in, 1, 2)
    w_ffd = _cast_pad(w_ff_down, D_FF_PAD, D_MODEL)
    w_router_p = jnp.pad(w_router, ((0, 0), (0, 0), (0, LANE - N_EXPERTS)))
    wgg_p = jnp.pad(w_gla_gate, ((0, 0), (0, LANE - GATE_RANK), (0, 0)))
    wggt_p = jnp.swapaxes(wgg_p, 1, 2)
    cache_k2 = cache_k.reshape(DEPTH, DEC_BATCH, PAST_LEN, hk)
    cache_v2 = cache_v.reshape(DEPTH, DEC_BATCH, PAST_LEN, hk)
    cache_kia = jnp.pad(cache_kidx, ((0, 0), (0, 0), (0, 0), (0, LANE - D_I)))
    cache_kib = jnp.pad(cache_kidx, ((0, 0), (0, 0), (0, 0), (LANE - D_I, 0)))
    zeros_conv = jnp.zeros((BATCH, CONV_W - 1, W_A), F32)
    zeros_lru = jnp.zeros((BATCH, W_A), F32)

    outs_p = [[] for _ in range(6)]
    outs_s = [[] for _ in range(6)]
    for l in range(DEPTH):
        proj = _in_proj(xb, w_t, l)

        lru_args = (conv_w[l], conv_b[l], w_rec_gate[l], b_rec_gate[l], w_in_gate[l], b_in_gate[l], lru_lambda[l])
        a_p, lru_p = _branch_a(proj, 0, BATCH, SEQ, 256, zeros_conv, zeros_lru, *lru_args)
        a_s, lru_s = _branch_a(proj, T_PROMPT, DEC_BATCH, DEC_SEQ, DEC_SEQ, state_conv[l], state_lru[l], *lru_args)

        b_p = _branch_b_prompt(proj)
        b_s = _branch_b(proj, T_PROMPT, DEC_BATCH, DEC_SEQ, DEC_SEQ, False,
                        cache=(cache_k2, cache_v2, cache_kia, cache_kib), layer=l)

        gla_args = (wgg_p[l], wggt_p[l], b_gla_gate[l], gla_norm_g[l])
        c_p, gla_p = _branch_c(proj, 0, BATCH, SEQ, 64, None, l, *gla_args)
        c_s, gla_s = _branch_c(proj, T_PROMPT, DEC_BATCH, DEC_SEQ, DEC_SEQ, state_gla, l, *gla_args)

        branches = jnp.stack([jnp.concatenate([a_p, a_s], axis=0), jnp.concatenate([b_p, b_s], axis=0),
                              jnp.concatenate([c_p, c_s], axis=0)])
        mixed = _merge(branches, w_branch, proj, b_branch_gate[l], l)
        y = _matmul(mixed, w_out, (l,), F32, tm=1024, tn=512, tk=D_MODEL)
        x, xb = _ln_residual(x, y, ln1_g[l], ln1_b[l])

        i = l // 2
        if l % 2 == 0:
            h = _glu_dense(xb, w_ff_gate, w_ff_up, i, D_FF_PAD)
            f = _matmul(h, w_ffd, (i,), F32, tm=1024, tn=1024, tk=D_FF_PAD // 4)
        else:
            combine = _router(x, w_router_p[i])
            f = _moe_routed(xb, combine, w_exp_gate, w_exp_up, w_exp_down, i)
        x, xb = _ln_residual(x, f, ln2_g[l], ln2_b[l])

        pp, ps = proj[:T_PROMPT], proj[T_PROMPT:]
        for dst, rows, nb, seq, lru_h, gla_st in ((outs_p, pp, BATCH, SEQ, lru_p, gla_p),
                                                  (outs_s, ps, DEC_BATCH, DEC_SEQ, lru_s, gla_s)):
            dst[0].append(rows[:, C_KB:C_KB + hk].reshape(nb, seq, N_KV, HD_B))
            dst[1].append(rows[:, C_VB:C_VB + hk].reshape(nb, seq, N_KV, HD_B))
            dst[2].append(rows[:, C_KIA:C_KIA + D_I].reshape(nb, seq, D_I))
            dst[3].append(lru_h)
            dst[4].append(rows[:, C_XA:C_XA + W_A].reshape(nb, seq, W_A)[:, seq - (CONV_W - 1):])
            dst[5].append(gla_st)

    k_p, v_p, ki_p, lru_po, conv_p, gla_po = [jnp.stack(o) for o in outs_p]
    k_s, v_s, ki_s, lru_so, conv_s, gla_so = [jnp.stack(o) for o in outs_s]
    return (x[:T_PROMPT].reshape(BATCH, SEQ, D_MODEL), x[T_PROMPT:].reshape(DEC_BATCH, DEC_SEQ, D_MODEL),
            k_p, v_p, ki_p, lru_po, conv_p, gla_po, k_s, v_s, ki_s, lru_so, conv_s, gla_so)
```

```python
import functools

import jax
import jax.numpy as jnp
from jax import lax
from jax.experimental import pallas as pl
from jax.experimental.pallas import tpu as pltpu

F32 = jnp.float32
BF16 = jnp.bfloat16
MXU_DTYPE = BF16
HIGHEST = lax.Precision.HIGHEST

D_MODEL = 4096
BATCH, SEQ = 4, 2048
DEPTH = 4
DEC_BATCH, DEC_SEQ = 32, 32
PAST_LEN = 1024
T_PROMPT = BATCH * SEQ
T_SAMPLE = DEC_BATCH * DEC_SEQ
T_ALL = T_PROMPT + T_SAMPLE
CHUNK = 64
W_BRANCH = 2048
W_A = W_BRANCH
NB_A = 16
BW_A = W_A // NB_A
CONV_W = 4
LRU_C = 8.0
H_B, HD_B, N_KV = 16, 128, 4
H_I, D_I = 16, 64
TOPK = 256
IDX_W_SCALE = (H_I ** -0.5) * (D_I ** -0.5)
H_C, DK_C, DV_C = 4, 256, 512
GATE_RANK = 16
GATE_NORM = 16.0
N_BRANCH = 3
D_FF = 11008
D_FF_PAD = 11264
N_EXPERTS = 8
D_FF_E = 7168
ALPHA = (2.0 * DEPTH) ** 0.25
LN_EPS = 1e-5
RMS_EPS = 1e-6
LANE = 128
INT_MIN = -2 ** 31

C_XA, C_GA, C_QB, C_KB, C_VB, C_QI = 0, 2048, 4096, 6144, 6656, 7168
C_QC, C_KC, C_VC, C_RC = 8192, 9216, 10240, 12288
C_KIA, C_KIB, C_WI, C_GL = 14336, 14464, 14592, 14720
C_GZ = 15360
N_PROJ = C_GZ + N_BRANCH * D_MODEL

VMEM_LIMIT = 56 * 1024 * 1024


def _params(sem):
    return pltpu.CompilerParams(dimension_semantics=sem, vmem_limit_bytes=VMEM_LIMIT)


def _sigmoid(x):
    return 1.0 / (1.0 + jnp.exp(-x))


def _log_sigmoid(x):
    return jnp.minimum(x, 0.0) - jnp.log(1.0 + jnp.exp(-jnp.abs(x)))


def _dot(a, b):
    return jnp.dot(a, b, preferred_element_type=F32)


def _dot_nt(a, b):
    return lax.dot_general(a, b, (((1,), (1,)), ((), ())), preferred_element_type=F32)


def _dot_tn(a, b):
    return lax.dot_general(a, b, (((0,), (0,)), ((), ())), preferred_element_type=F32)


def _mm_body(x_ref, w_ref, o_ref, acc_ref):
    k = pl.program_id(2)

    @pl.when(k == 0)
    def _init():
        acc_ref[...] = jnp.zeros_like(acc_ref)

    acc_ref[...] += _dot(x_ref[...].astype(MXU_DTYPE), w_ref[...].astype(MXU_DTYPE))

    @pl.when(k == pl.num_programs(2) - 1)
    def _fin():
        o_ref[...] = acc_ref[...].astype(o_ref.dtype)


def _mm_full_k_body(x_ref, w_ref, o_ref):
    o_ref[...] = _dot(x_ref[...].astype(MXU_DTYPE), w_ref[...].astype(MXU_DTYPE)).astype(o_ref.dtype)


def _matmul(x, w, lead, out_dtype, tm, tn, tk):
    m_dim, k_dim = x.shape
    n_dim = w.shape[-1]
    nl = len(lead)
    full_k = tk == k_dim
    return pl.pallas_call(
        _mm_full_k_body if full_k else _mm_body,
        grid=(m_dim // tm, n_dim // tn, k_dim // tk),
        in_specs=[pl.BlockSpec((tm, tk), lambda m, n, k: (m, k)),
                  pl.BlockSpec((None,) * nl + (tk, tn), lambda m, n, k: lead + (k, n))],
        out_specs=pl.BlockSpec((tm, tn), lambda m, n, k: (m, n)),
        out_shape=jax.ShapeDtypeStruct((m_dim, n_dim), out_dtype),
        scratch_shapes=[] if full_k else [pltpu.VMEM((tm, tn), F32)],
        compiler_params=_params(("parallel", "parallel", "arbitrary")),
    )(x, w)


def _cast_pad_body(x_ref, o_ref, *, rows, cols):
    tr, tc = o_ref.shape
    r = pl.program_id(1) * tr + lax.broadcasted_iota(jnp.int32, (tr, tc), 0)
    c = pl.program_id(2) * tc + lax.broadcasted_iota(jnp.int32, (tr, tc), 1)
    o_ref[...] = jnp.where((r < rows) & (c < cols), x_ref[...], 0.0).astype(o_ref.dtype)


def _cast_pad(w, rows_pad, cols_pad, tr=512, tc=1024):
    n_l, rows, cols = w.shape
    assert (rows_pad - rows) < tr and (cols_pad - cols) < tc
    spec = pl.BlockSpec((None, tr, tc), lambda l, i, j: (l, i, j))
    return pl.pallas_call(
        functools.partial(_cast_pad_body, rows=rows, cols=cols),
        grid=(n_l, rows_pad // tr, cols_pad // tc),
        in_specs=[spec], out_specs=spec,
        out_shape=jax.ShapeDtypeStruct((n_l, rows_pad, cols_pad), BF16),
        compiler_params=_params(("parallel", "parallel", "parallel")),
    )(w)


def _ln_body(x_ref, y_ref, g_ref, b_ref, o_ref, ob_ref):
    s = ALPHA * x_ref[...] + y_ref[...]
    mu = jnp.mean(s, axis=-1, keepdims=True)
    d = s - mu
    var = jnp.mean(d * d, axis=-1, keepdims=True)
    o = d * lax.rsqrt(var + LN_EPS) * g_ref[...] + b_ref[...]
    o_ref[...] = o
    ob_ref[...] = o.astype(BF16)


def _ln_residual(x, y, g, b, tm=256):
    t_dim = x.shape[0]
    row = pl.BlockSpec((tm, D_MODEL), lambda m: (m, 0))
    vec = pl.BlockSpec((1, D_MODEL), lambda m: (0, 0))
    return pl.pallas_call(
        _ln_body,
        grid=(t_dim // tm,),
        in_specs=[row, row, vec, vec],
        out_specs=[row, row],
        out_shape=[jax.ShapeDtypeStruct((t_dim, D_MODEL), F32), jax.ShapeDtypeStruct((t_dim, D_MODEL), BF16)],
        compiler_params=_params(("parallel",)),
    )(x, y, g.reshape(1, D_MODEL), b.reshape(1, D_MODEL))


def _glu_body(x_ref, wg_ref, wu_ref, o_ref, *, n_live):
    n = pl.program_id(1)

    @pl.when(n < n_live)
    def _live():
        x = x_ref[...]
        g = _dot(x, wg_ref[...].astype(MXU_DTYPE))
        u = _dot(x, wu_ref[...].astype(MXU_DTYPE))
        o_ref[...] = (g * _sigmoid(g) * u).astype(o_ref.dtype)

    @pl.when(n >= n_live)
    def _pad():
        o_ref[...] = jnp.zeros_like(o_ref)


def _glu_dense(xb, wg, wu, layer, n_pad, tm=1024, tn=256):
    t_dim, k_dim = xb.shape
    n_live = wg.shape[-1] // tn
    assert n_live * tn == wg.shape[-1] and n_pad % tn == 0
    wspec = pl.BlockSpec((None, k_dim, tn), lambda m, n: (layer, 0, jnp.minimum(n, n_live - 1)))
    return pl.pallas_call(
        functools.partial(_glu_body, n_live=n_live),
        grid=(t_dim // tm, n_pad // tn),
        in_specs=[pl.BlockSpec((tm, k_dim), lambda m, n: (m, 0)), wspec, wspec],
        out_specs=pl.BlockSpec((tm, tn), lambda m, n: (m, n)),
        out_shape=jax.ShapeDtypeStruct((t_dim, n_pad), BF16),
        compiler_params=_params(("parallel", "arbitrary")),
    )(xb, wg, wu)


MOE_TM = 1024
MOE_G = 256
MOE_SRC = 512


def _moe_plan(combine, n_exp, tm, g, src):
    i32 = jnp.int32
    t_dim = combine.shape[0]
    n_rows = 2 * t_dim + n_exp * tm
    mask = combine[:, :n_exp] > 0.0
    mi = mask.astype(i32)
    cum = jnp.cumsum(mi, axis=0)
    cnt = cum[-1]
    gs = (cnt + tm - 1) // tm * tm
    g_end = jnp.cumsum(gs)
    g0 = g_end - gs
    rowid = jnp.where(mask, g0[None, :] + cum - mi, -1).astype(i32)
    used_rows = g_end[-1]

    n_mt = n_rows // tm
    mt_used = used_rows // tm
    mt_row0 = jnp.arange(n_mt, dtype=i32) * tm
    mt_exp = jnp.minimum(jnp.searchsorted(g_end, mt_row0, side="right"), n_exp - 1)
    mt_live = jnp.clip(cnt[mt_exp] - (mt_row0 - g0[mt_exp]), 0, tm)

    n_gt = n_rows // g
    n_sb = t_dim // src
    gt_used = used_rows // g
    r0 = jnp.arange(n_gt, dtype=i32) * g
    gt_exp = jnp.minimum(jnp.searchsorted(g_end, r0, side="right"), n_exp - 1).astype(i32)
    rank0 = r0 - g0[gt_exp]
    rank1 = jnp.minimum(rank0 + g, cnt[gt_exp]) - 1
    cb = cum[src - 1::src].T
    cb_t = cb[gt_exp]
    fb = jnp.sum(cb_t <= rank0[:, None], axis=1)
    lb = jnp.sum(cb_t <= rank1[:, None], axis=1)
    has_rows = rank1 >= rank0
    fb = jnp.where(has_rows, jnp.minimum(fb, n_sb - 1), 0)
    lb = jnp.where(has_rows, jnp.minimum(lb, n_sb - 1), 0)
    nblk = jnp.where(jnp.arange(n_gt) < gt_used, lb - fb + 1, 0)
    off_end = jnp.cumsum(nblk)
    off = off_end - nblk
    n_items = n_gt + n_exp * n_sb
    w = jnp.arange(n_items, dtype=i32)
    it_valid = w < off_end[-1]
    it_tile = jnp.minimum(jnp.searchsorted(off_end, w, side="right"), jnp.maximum(gt_used - 1, 0)).astype(i32)
    it_blk = jnp.clip(fb[it_tile] + w - off[it_tile], 0, lb[it_tile]).astype(i32)
    it_first = (w == off[it_tile])
    gather = jnp.stack([it_tile, it_blk, gt_exp[it_tile], it_first.astype(i32), it_valid.astype(i32)])

    ce = cum[g - 1::g]
    cs = jnp.concatenate([jnp.zeros((1, n_exp), i32), ce[:-1]], axis=0)
    a = g0[None, :] + cs
    b = g0[None, :] + ce
    blk0 = jnp.clip(a // g, 0, n_gt - 1)
    blk1 = jnp.clip((b - 1) // g, 0, n_gt - 1)
    v0 = b > a
    v1 = v0 & (blk1 > blk0)
    v1f, blk1f = v1.reshape(-1), blk1.reshape(-1)
    last_used = lax.cummax(jnp.where(v1f, jnp.arange(v1f.shape[0], dtype=i32), -1))
    blk1f = jnp.where(last_used >= 0, blk1f[jnp.maximum(last_used, 0)], 0)
    c_blk = jnp.stack([blk0.reshape(-1), blk1f], axis=-1).reshape(-1).astype(i32)
    c_val = jnp.stack([v0, v1], axis=-1).reshape(-1).astype(i32)
    return dict(n_rows=n_rows, rowid=rowid, mt_used=mt_used.reshape(1).astype(i32), mt_exp=mt_exp.astype(i32),
                mt_live=mt_live.astype(i32), gather=gather, c_blk=c_blk, c_val=c_val)


def _moe_gather_body(it_ref, x_ref, rid_ref, o_ref, *, g):
    w = pl.program_id(0)
    tile, e, first, valid = it_ref[0, w], it_ref[2, w], it_ref[3, w], it_ref[4, w]

    @pl.when(valid == 1)
    def _():
        rid = rid_ref[pl.ds(e, 1), :]
        rows = tile * g + lax.broadcasted_iota(jnp.int32, (g, rid.shape[1]), 0)
        part = _dot((rid == rows).astype(MXU_DTYPE), x_ref[...]).astype(o_ref.dtype)

        @pl.when(first == 1)
        def _set():
            o_ref[...] = part

        @pl.when(first == 0)
        def _add():
            o_ref[...] += part


def _moe_gather(xb, rowid_t, plan, g, src):
    d = xb.shape[1]
    items = plan["gather"]
    return pl.pallas_call(
        functools.partial(_moe_gather_body, g=g),
        grid_spec=pltpu.PrefetchScalarGridSpec(
            num_scalar_prefetch=1, grid=(items.shape[1],),
            in_specs=[pl.BlockSpec((src, d), lambda w, it: (it[1, w], 0)),
                      pl.BlockSpec((rowid_t.shape[0], src), lambda w, it: (0, it[1, w]))],
            out_specs=pl.BlockSpec((g, d), lambda w, it: (it[0, w], 0))),
        out_shape=jax.ShapeDtypeStruct((plan["n_rows"], d), xb.dtype),
        compiler_params=_params(("arbitrary",)),
    )(items, xb, rowid_t)


def _moe_up_body(used_ref, exp_ref, live_ref, x_ref, wg_ref, wu_ref, o_ref):
    i = pl.program_id(0)
    half = x_ref.shape[0] // 2

    def run(rows):
        x = x_ref[0:rows, :]
        gate = _dot(x, wg_ref[...].astype(MXU_DTYPE))
        up = _dot(x, wu_ref[...].astype(MXU_DTYPE))
        o_ref[0:rows, :] = (gate * _sigmoid(gate) * up).astype(o_ref.dtype)

    @pl.when((i < used_ref[0]) & (live_ref[i] > half))
    def _full():
        run(2 * half)

    @pl.when((i < used_ref[0]) & (live_ref[i] <= half))
    def _half():
        run(half)


def _grouped_index(nn, nk):
    def pick(i, n, k, used):
        live = i < used[0]
        last = jnp.maximum(used[0] - 1, 0)
        return jnp.where(live, i, last), jnp.where(live, n, nn - 1), jnp.where(live, k, nk - 1)
    return pick


def _moe_up(xg, wg, wu, layer, plan, tm, tn):
    n_rows, d = xg.shape
    f = wg.shape[-1]
    nn = f // tn
    pick = _grouped_index(nn, 1)

    def x_map(i, n, used, exp, live):
        return pick(i, n, 0, used)[0], 0

    def w_map(i, n, used, exp, live):
        ii, n2, _ = pick(i, n, 0, used)
        return layer, exp[ii], 0, n2

    def o_map(i, n, used, exp, live):
        return pick(i, n, 0, used)[:2]

    wspec = pl.BlockSpec((None, None, d, tn), w_map)
    return pl.pallas_call(
        _moe_up_body,
        grid_spec=pltpu.PrefetchScalarGridSpec(
            num_scalar_prefetch=3, grid=(n_rows // tm, nn),
            in_specs=[pl.BlockSpec((tm, d), x_map), wspec, wspec],
            out_specs=pl.BlockSpec((tm, tn), o_map)),
        out_shape=jax.ShapeDtypeStruct((n_rows, f), xg.dtype),
        compiler_params=_params(("arbitrary", "arbitrary")),
    )(plan["mt_used"], plan["mt_exp"], plan["mt_live"], xg, wg, wu)


def _moe_down_body(used_ref, exp_ref, live_ref, x_ref, w_ref, hi_ref, lo_ref):
    i = pl.program_id(0)
    half = x_ref.shape[0] // 2

    def run(rows):
        y = _dot(x_ref[0:rows, :], w_ref[...].astype(MXU_DTYPE))
        hi = y.astype(hi_ref.dtype)
        hi_ref[0:rows, :] = hi
        lo_ref[0:rows, :] = (y - hi.astype(F32)).astype(lo_ref.dtype)

    @pl.when((i < used_ref[0]) & (live_ref[i] > half))
    def _full():
        run(2 * half)

    @pl.when((i < used_ref[0]) & (live_ref[i] <= half))
    def _half():
        run(half)


def _moe_down(h, wd, layer, plan, tm, tn):
    n_rows, f = h.shape
    d = wd.shape[-1]
    nn = d // tn
    pick = _grouped_index(nn, 1)

    def x_map(i, n, used, exp, live):
        return pick(i, n, 0, used)[0], 0

    def w_map(i, n, used, exp, live):
        ii, n2, _ = pick(i, n, 0, used)
        return layer, exp[ii], 0, n2

    def o_map(i, n, used, exp, live):
        return pick(i, n, 0, used)[:2]

    piece = jax.ShapeDtypeStruct((n_rows, d), h.dtype)
    return pl.pallas_call(
        _moe_down_body,
        grid_spec=pltpu.PrefetchScalarGridSpec(
            num_scalar_prefetch=3, grid=(n_rows // tm, nn),
            in_specs=[pl.BlockSpec((tm, f), x_map), pl.BlockSpec((None, None, f, tn), w_map)],
            out_specs=[pl.BlockSpec((tm, tn), o_map), pl.BlockSpec((tm, tn), o_map)]),
        out_shape=[piece, piece],
        compiler_params=_params(("arbitrary", "arbitrary")),
    )(plan["mt_used"], plan["mt_exp"], plan["mt_live"], h, wd)


def _moe_combine_body(blk_ref, val_ref, hi0_ref, lo0_ref, hi1_ref, lo1_ref, rid_ref, c_ref, o_ref, *, g, n_exp):
    m, e = pl.program_id(0), pl.program_id(1)
    item = (m * n_exp + e) * 2

    @pl.when(e == 0)
    def _init():
        o_ref[...] = jnp.zeros_like(o_ref)

    @pl.when(val_ref[item] == 1)
    def _():
        rid = rid_ref[...]
        rid_e = jnp.sum(jnp.where(lax.broadcasted_iota(jnp.int32, rid.shape, 1) == e, rid, 0), axis=1, keepdims=True)
        c = c_ref[...]
        c_e = jnp.sum(jnp.where(lax.broadcasted_iota(jnp.int32, c.shape, 1) == e, c, 0.0), axis=1, keepdims=True)
        lanes = lax.broadcasted_iota(jnp.int32, (g, g), 1)

        def gathered(blk, hi_ref, lo_ref):
            onehot = (rid_e == blk * g + lanes).astype(MXU_DTYPE)
            return _dot(onehot, hi_ref[...]) + _dot(onehot, lo_ref[...])

        o_ref[...] += c_e * gathered(blk_ref[item], hi0_ref, lo0_ref)

        @pl.when(val_ref[item + 1] == 1)
        def _second():
            o_ref[...] += c_e * gathered(blk_ref[item + 1], hi1_ref, lo1_ref)


def _moe_combine(y_hi, y_lo, rowid, combine, plan, g):
    t_dim, n_exp = rowid.shape
    d = y_hi.shape[1]

    def y_map(which):
        return lambda m, e, blk, val: (blk[(m * n_exp + e) * 2 + which], 0)

    tile = lambda m, e, blk, val: (m, 0)
    return pl.pallas_call(
        functools.partial(_moe_combine_body, g=g, n_exp=n_exp),
        grid_spec=pltpu.PrefetchScalarGridSpec(
            num_scalar_prefetch=2, grid=(t_dim // g, n_exp),
            in_specs=[pl.BlockSpec((g, d), y_map(0)), pl.BlockSpec((g, d), y_map(0)),
                      pl.BlockSpec((g, d), y_map(1)), pl.BlockSpec((g, d), y_map(1)),
                      pl.BlockSpec((g, n_exp), tile), pl.BlockSpec((g, combine.shape[1]), tile)],
            out_specs=pl.BlockSpec((g, d), tile)),
        out_shape=jax.ShapeDtypeStruct((t_dim, d), F32),
        compiler_params=_params(("arbitrary", "arbitrary")),
    )(plan["c_blk"], plan["c_val"], y_hi, y_lo, y_hi, y_lo, rowid, combine)


def _moe_routed(xb, combine, wg, wu, wd, layer, tm=MOE_TM, g=MOE_G, src=MOE_SRC, tn_up=256, tn_down=256):
    n_exp = wg.shape[1]
    plan = _moe_plan(combine, n_exp, tm, g, src)
    xg = _moe_gather(xb, plan["rowid"].T, plan, g, src)
    h = _moe_up(xg, wg, wu, layer, plan, tm, tn_up)
    y_hi, y_lo = _moe_down(h, wd, layer, plan, tm, tn_down)
    return _moe_combine(y_hi, y_lo, plan["rowid"], combine, plan, g)


def _router_body(x_ref, w_ref, c_ref):
    logits = jnp.dot(x_ref[...], w_ref[...], preferred_element_type=F32, precision=HIGHEST)
    lane = lax.broadcasted_iota(jnp.int32, logits.shape, 1)
    logits = jnp.where(lane < N_EXPERTS, logits, -jnp.inf)
    m1 = jnp.max(logits, axis=1, keepdims=True)
    i1 = jnp.min(jnp.where(logits == m1, lane, LANE), axis=1, keepdims=True)
    rest = jnp.where(lane == i1, -jnp.inf, logits)
    m2 = jnp.max(rest, axis=1, keepdims=True)
    i2 = jnp.min(jnp.where(rest == m2, lane, LANE), axis=1, keepdims=True)
    e2 = jnp.exp(m2 - m1)
    w1 = 1.0 / (1.0 + e2)
    w2 = e2 / (1.0 + e2)
    c_ref[...] = jnp.where(lane == i1, w1, 0.0) + jnp.where(lane == i2, w2, 0.0)


def _router(x, w_pad, tm=512):
    t_dim = x.shape[0]
    return pl.pallas_call(
        _router_body,
        grid=(t_dim // tm,),
        in_specs=[pl.BlockSpec((tm, D_MODEL), lambda m: (m, 0)),
                  pl.BlockSpec((D_MODEL, LANE), lambda m: (0, 0))],
        out_specs=pl.BlockSpec((tm, LANE), lambda m: (m, 0)),
        out_shape=jax.ShapeDtypeStruct((t_dim, LANE), F32),
        compiler_params=_params(("parallel",)),
    )(x, w_pad)


def _merge_body(br_ref, w_ref, gz_ref, bg_ref, o_ref, acc_ref):
    j = pl.program_id(2)

    @pl.when(j == 0)
    def _init():
        acc_ref[...] = jnp.zeros_like(acc_ref)

    pj = _dot(br_ref[...].astype(MXU_DTYPE), w_ref[...].astype(MXU_DTYPE))
    acc_ref[...] += _sigmoid(gz_ref[...] + bg_ref[...]) * pj

    @pl.when(j == N_BRANCH - 1)
    def _fin():
        o_ref[...] = acc_ref[...].astype(o_ref.dtype)


def _merge(branches, w_branch, proj, b_gate, layer, tm=1024, tn=1024):
    nt = D_MODEL // tn
    gz_blk = C_GZ // tn
    return pl.pallas_call(
        _merge_body,
        grid=(T_ALL // tm, nt, N_BRANCH),
        in_specs=[pl.BlockSpec((None, tm, W_BRANCH), lambda m, n, j: (j, m, 0)),
                  pl.BlockSpec((None, None, W_BRANCH, tn), lambda m, n, j: (layer, j, 0, n)),
                  pl.BlockSpec((tm, tn), lambda m, n, j: (m, gz_blk + j * nt + n)),
                  pl.BlockSpec((1, tn), lambda m, n, j: (0, j * nt + n))],
        out_specs=pl.BlockSpec((tm, tn), lambda m, n, j: (m, n)),
        out_shape=jax.ShapeDtypeStruct((T_ALL, D_MODEL), BF16),
        scratch_shapes=[pltpu.VMEM((tm, tn), F32)],
        compiler_params=_params(("parallel", "parallel", "arbitrary")),
    )(branches, w_branch, proj, b_gate.reshape(1, N_BRANCH * D_MODEL))


def _lru_body(xa_ref, ga_ref, buf_ref, h0_ref, cw_ref, cb_ref, wrg_ref, brg_ref, wig_ref, big_ref, lam_ref,
              o_ref, hlast_ref, xp_ref, a_ref, u_ref, h_ref, *, tl):
    t = pl.program_id(1)

    @pl.when(t == 0)
    def _init():
        xp_ref[5:8, :] = buf_ref[...]
        h_ref[...] = h0_ref[...]

    xp_ref[8:8 + tl, :] = xa_ref[...]
    xc = cb_ref[...] + xp_ref[8:8 + tl, :] * cw_ref[3:4, :]
    for j in range(CONV_W - 1):
        xc = xc + xp_ref[5 + j:5 + j + tl, :] * cw_ref[j:j + 1, :]
    xp_ref[5:8, :] = xa_ref[tl - 3:tl, :]

    xcb = xc.astype(MXU_DTYPE)
    for n in range(NB_A):
        sl = slice(n * BW_A, (n + 1) * BW_A)
        xs = xcb[:, sl]
        r = _sigmoid(_dot(xs, wrg_ref[n].astype(MXU_DTYPE)) + brg_ref[:, sl])
        i = _sigmoid(_dot(xs, wig_ref[n].astype(MXU_DTYPE)) + big_ref[:, sl])
        lam = lam_ref[:, sl]
        softplus = jnp.maximum(-lam, 0.0) + jnp.log(1.0 + jnp.exp(-jnp.abs(lam)))
        log_a = -LRU_C * r * softplus
        a_ref[:, sl] = jnp.exp(log_a)
        u_ref[:, sl] = jnp.sqrt(1.0 - jnp.exp(2.0 * log_a)) * (i * xc[:, sl])

    def step(s, h):
        h = a_ref[pl.ds(s, 1), :] * h + u_ref[pl.ds(s, 1), :]
        a_ref[pl.ds(s, 1), :] = h
        return h

    h = lax.fori_loop(0, tl, step, h_ref[...], unroll=8)
    h_ref[...] = h
    hlast_ref[...] = h
    g = ga_ref[...]
    gelu = 0.5 * g * (1.0 + jnp.tanh(0.7978845608028654 * (g + 0.044715 * (g * g * g))))
    o_ref[...] = (a_ref[...] * gelu).astype(o_ref.dtype)


def _branch_a(proj, row0, nb, seq, tl, conv_buf, h0, cw, cb, wrg, brg, wig, big, lam):
    nt = seq // tl
    rb0 = row0 // tl
    vec = pl.BlockSpec((1, W_A), lambda b, t: (0, 0))
    blk = pl.BlockSpec((NB_A, BW_A, BW_A), lambda b, t: (0, 0, 0))
    out, h_last = pl.pallas_call(
        functools.partial(_lru_body, tl=tl),
        grid=(nb, nt),
        in_specs=[pl.BlockSpec((tl, W_A), lambda b, t: (rb0 + b * nt + t, C_XA // W_A)),
                  pl.BlockSpec((tl, W_A), lambda b, t: (rb0 + b * nt + t, C_GA // W_A)),
                  pl.BlockSpec((None, CONV_W - 1, W_A), lambda b, t: (b, 0, 0)),
                  pl.BlockSpec((None, 1, W_A), lambda b, t: (b, 0, 0)),
                  pl.BlockSpec((CONV_W, W_A), lambda b, t: (0, 0)),
                  vec, blk, vec, blk, vec, vec],
        out_specs=[pl.BlockSpec((tl, W_A), lambda b, t: (b * nt + t, 0)),
                   pl.BlockSpec((None, 1, W_A), lambda b, t: (b, 0, 0))],
        out_shape=[jax.ShapeDtypeStruct((nb * seq, W_A), BF16), jax.ShapeDtypeStruct((nb, 1, W_A), F32)],
        scratch_shapes=[pltpu.VMEM((tl + 8, W_A), F32), pltpu.VMEM((tl, W_A), F32), pltpu.VMEM((tl, W_A), F32),
                        pltpu.VMEM((1, W_A), F32)],
        compiler_params=_params(("parallel", "arbitrary")),
    )(proj, proj, conv_buf, h0.reshape(nb, 1, W_A), cw, cb.reshape(1, W_A), wrg, brg.reshape(1, W_A),
      wig, big.reshape(1, W_A), lam.reshape(1, W_A))
    return out, h_last.reshape(nb, W_A)


def _gla_body(*refs, ck, has_s0):
    if has_s0:
        q_ref, k_ref, v_ref, rc_ref, gl_ref, wgg_ref, wggt_ref, bgr_ref, bgc_ref, g_ref, s0_ref, o_ref, s_ref = refs
    else:
        q_ref, k_ref, v_ref, rc_ref, gl_ref, wgg_ref, wggt_ref, bgr_ref, bgc_ref, g_ref, o_ref, s_ref = refs
    c = pl.program_id(1)

    @pl.when(c == 0)
    def _init():
        s_ref[...] = s0_ref[...] if has_s0 else jnp.zeros_like(s_ref)

    gl = gl_ref[...].astype(MXU_DTYPE)
    lg = _log_sigmoid(_dot(gl, wgg_ref[...].astype(MXU_DTYPE)) + bgr_ref[...]) / GATE_NORM
    lg_t = _log_sigmoid(_dot_nt(wggt_ref[...].astype(MXU_DTYPE), gl) + bgc_ref[...]) / GATE_NORM
    row = lax.broadcasted_iota(jnp.int32, (ck, ck), 0)
    col = lax.broadcasted_iota(jnp.int32, (ck, ck), 1)
    tri = row >= col
    bcum_all = jnp.dot(tri.astype(F32), lg, preferred_element_type=F32, precision=HIGHEST)
    b_last_col_all = jnp.sum(lg_t, axis=1, keepdims=True)

    for h in range(H_C):
        ks = slice(h * DK_C, (h + 1) * DK_C)
        vs = slice(h * DV_C, (h + 1) * DV_C)
        bcum = bcum_all[:, ks]
        b_last = bcum[ck - 1:ck, :]
        k = k_ref[:, ks]
        qe = (q_ref[:, ks] * (DK_C ** -0.5) * jnp.exp(bcum)).astype(MXU_DTYPE)
        ke = (k * jnp.exp(-bcum)).astype(MXU_DTYPE)
        kt = (k * jnp.exp(b_last - bcum)).astype(MXU_DTYPE)
        vb = v_ref[:, vs].astype(MXU_DTYPE)
        att = jnp.where(tri, _dot_nt(qe, ke), 0.0)
        s = s_ref[h]
        o = _dot(att.astype(MXU_DTYPE), vb) + _dot(qe, s.astype(MXU_DTYPE))
        s_ref[h] = jnp.exp(b_last_col_all[ks, :]) * s + _dot_tn(kt, vb)

        o = o * lax.rsqrt(jnp.mean(o * o, axis=-1, keepdims=True) + RMS_EPS) * g_ref[...]
        rc = rc_ref[:, vs]
        o_ref[:, vs] = (o * (rc * _sigmoid(rc))).astype(o_ref.dtype)


def _branch_c(proj, row0, nb, seq, ck, s0, layer, wgg_pad, wggt_pad, bgg, gla_g):
    nc = seq // ck
    rb0 = row0 // ck
    hk, hv = H_C * DK_C, H_C * DV_C

    def rows(cb):
        return lambda b, c: (rb0 + b * nc + c, cb)

    def const(b, c):
        return (0, 0)

    state = pl.BlockSpec((None, H_C, DK_C, DV_C), lambda b, c: (b, 0, 0, 0))
    has_s0 = s0 is not None
    s0_spec = [pl.BlockSpec((None, None, H_C, DK_C, DV_C), lambda b, c: (layer, b, 0, 0, 0))] if has_s0 else []
    s0_arg = [s0] if has_s0 else []
    out, s_fin = pl.pallas_call(
        functools.partial(_gla_body, ck=ck, has_s0=has_s0),
        grid=(nb, nc),
        in_specs=[pl.BlockSpec((ck, hk), rows(C_QC // hk)),
                  pl.BlockSpec((ck, hk), rows(C_KC // hk)),
                  pl.BlockSpec((ck, hv), rows(C_VC // hv)),
                  pl.BlockSpec((ck, hv), rows(C_RC // hv)),
                  pl.BlockSpec((ck, LANE), rows(C_GL // LANE)),
                  pl.BlockSpec((LANE, hk), const),
                  pl.BlockSpec((hk, LANE), const),
                  pl.BlockSpec((1, hk), const),
                  pl.BlockSpec((hk, 1), const),
                  pl.BlockSpec((1, DV_C), const)] + s0_spec,
        out_specs=[pl.BlockSpec((ck, hv), lambda b, c: (b * nc + c, 0)), state],
        out_shape=[jax.ShapeDtypeStruct((nb * seq, hv), BF16),
                   jax.ShapeDtypeStruct((nb, H_C, DK_C, DV_C), F32)],
        compiler_params=_params(("parallel", "arbitrary")),
    )(proj, proj, proj, proj, proj, wgg_pad, wggt_pad, bgg.reshape(1, hk), bgg.reshape(hk, 1),
      gla_g.reshape(1, DV_C), *s0_arg)
    return out, s_fin


def _dsa_body(*refs, tq, lc, lk, lpad, chunked, qt0):
    if lc:
        (q_ref, qi_ref, wi_ref, kn_ref, vn_ref, kan_ref, kbn_ref, kc_ref, vc_ref, kac_ref, kbc_ref,
         o_ref, k_s, v_s, ka_s, kb_s, key_s, u_s) = refs
    else:
        (q_ref, qi_ref, wi_ref, kn_ref, vn_ref, kan_ref, kbn_ref,
         o_ref, k_s, v_s, ka_s, kb_s, key_s, u_s) = refs
    qt = qt0 + pl.program_id(1)
    n_keys = lc + lk

    @pl.when(pl.program_id(1) == 0)
    def _stage_keys():
        if lc:
            k_s[0:lc, :] = kc_ref[...].astype(MXU_DTYPE)
            v_s[0:lc, :] = vc_ref[...].astype(MXU_DTYPE)
            ka_s[0:lc, :] = kac_ref[...].astype(MXU_DTYPE)
            kb_s[0:lc, :] = kbc_ref[...].astype(MXU_DTYPE)
        k_s[lc:n_keys, :] = kn_ref[0:lk, :].astype(MXU_DTYPE)
        v_s[lc:n_keys, :] = vn_ref[0:lk, :].astype(MXU_DTYPE)
        ka_s[lc:n_keys, :] = kan_ref[0:lk, :].astype(MXU_DTYPE)
        kb_s[lc:n_keys, :] = kbn_ref[0:lk, :].astype(MXU_DTYPE)
        if lpad > n_keys:
            k_s[n_keys:lpad, :] = jnp.zeros((lpad - n_keys, N_KV * HD_B), MXU_DTYPE)
            v_s[n_keys:lpad, :] = jnp.zeros((lpad - n_keys, N_KV * HD_B), MXU_DTYPE)
            ka_s[n_keys:lpad, :] = jnp.zeros((lpad - n_keys, LANE), MXU_DTYPE)
            kb_s[n_keys:lpad, :] = jnp.zeros((lpad - n_keys, LANE), MXU_DTYPE)
        for r0 in range(0, lpad, LANE):
            rr = r0 + lax.broadcasted_iota(jnp.int32, (LANE, lpad), 0)
            cc = lax.broadcasted_iota(jnp.int32, (LANE, lpad), 1)
            u_s[r0:r0 + LANE, :] = (rr < cc).astype(MXU_DTYPE)

    qi = qi_ref[...].astype(MXU_DTYPE)
    wi = wi_ref[...] * IDX_W_SCALE
    ka = ka_s[...]
    kb = kb_s[...]
    score = jnp.zeros((tq, lpad), F32)
    for p in range(H_I // 2):
        qp = qi[:, p * LANE:(p + 1) * LANE]
        score = score + wi[:, 2 * p:2 * p + 1] * jnp.maximum(_dot_nt(qp, ka), 0.0)
        score = score + wi[:, 2 * p + 1:2 * p + 2] * jnp.maximum(_dot_nt(qp, kb), 0.0)

    score = jnp.where(score == 0.0, 0.0, score)
    bits = lax.bitcast_convert_type(score, jnp.int32)
    key = bits ^ ((bits >> 31) & 0x7FFFFFFF)
    col = lax.broadcasted_iota(jnp.int32, (tq, lpad), 1)
    if chunked:
        pos = qt * tq + lax.broadcasted_iota(jnp.int32, (tq, lpad), 0)
        valid = col < (pos // CHUNK + 1) * CHUNK
    else:
        valid = col < n_keys
    key_s[...] = jnp.where(valid, key, INT_MIN)

    n_grp = 2 if tq >= LANE else 1
    rows_g = tq // n_grp

    def count_ge(grp, cand):
        keys_g = key_s[grp * rows_g:(grp + 1) * rows_g, :]
        return jnp.sum((keys_g >= cand).astype(jnp.int32), axis=1, keepdims=True)

    zero = jnp.zeros((rows_g, 1), jnp.int32)
    prefixes = tuple(jnp.where(count_ge(grp, zero) >= TOPK, 0, INT_MIN).astype(jnp.int32) for grp in range(n_grp))

    def search(i, prefixes):
        bit = jnp.left_shift(jnp.int32(1), 30 - i)
        cands = [p | bit for p in prefixes]
        return tuple(jnp.where(count_ge(grp, c) >= TOPK, c, p) for grp, (c, p) in enumerate(zip(cands, prefixes)))

    prefix = jnp.concatenate(lax.fori_loop(0, 31, search, prefixes), axis=0)

    keys = key_s[...]
    above = keys > prefix
    equal = (keys == prefix) & valid
    n_above = jnp.sum(above.astype(jnp.int32), axis=1, keepdims=True)
    rank = _dot(equal.astype(MXU_DTYPE), u_s[...])
    keep = above | (equal & (rank < (TOPK - n_above).astype(F32)))
    bias = jnp.where(keep, 0.0, -jnp.inf)

    q = q_ref[...]
    n_rep = H_B // N_KV
    bias_g = jnp.concatenate([bias] * n_rep, axis=0)
    for n in range(N_KV):
        kn = k_s[:, n * HD_B:(n + 1) * HD_B]
        vn = v_s[:, n * HD_B:(n + 1) * HD_B]
        qg = jnp.concatenate([q[:, (n * n_rep + g) * HD_B:(n * n_rep + g + 1) * HD_B] for g in range(n_rep)], axis=0)
        s = _dot_nt(qg.astype(MXU_DTYPE), kn) * (HD_B ** -0.5) + bias_g
        m = jnp.max(s, axis=1, keepdims=True)
        p = jnp.exp(s - m)
        den = jnp.sum(p, axis=1, keepdims=True)
        o = _dot(p.astype(MXU_DTYPE), vn) / den
        for g in range(n_rep):
            sl = slice((n * n_rep + g) * HD_B, (n * n_rep + g + 1) * HD_B)
            o_ref[:, sl] = o[g * tq:(g + 1) * tq, :].astype(o_ref.dtype)


def _branch_b(proj, row0, nb, seq, tq, chunked, cache=None, layer=0, qt0=0, n_qt=None):
    nq = seq // tq
    n_qt = nq if n_qt is None else n_qt
    rq0 = row0 // tq
    rk0 = row0 // seq
    lc = 0 if cache is None else cache[0].shape[2]
    lk = (qt0 + n_qt) * tq if chunked else seq
    lpad = -(-(lc + lk) // LANE) * LANE
    hk = N_KV * HD_B

    def qrows(cb):
        return lambda b, t: (rq0 + b * nq + qt0 + t, cb)

    def krows(cb):
        return lambda b, t: (rk0 + b, cb)

    in_specs = [pl.BlockSpec((tq, H_B * HD_B), qrows(C_QB // (H_B * HD_B))),
                pl.BlockSpec((tq, H_I * D_I), qrows(C_QI // (H_I * D_I))),
                pl.BlockSpec((tq, LANE), qrows(C_WI // LANE)),
                pl.BlockSpec((seq, hk), krows(C_KB // hk)),
                pl.BlockSpec((seq, hk), krows(C_VB // hk)),
                pl.BlockSpec((seq, LANE), krows(C_KIA // LANE)),
                pl.BlockSpec((seq, LANE), krows(C_KIB // LANE))]
    args = [proj] * 7
    if cache is not None:
        in_specs += [pl.BlockSpec((None, None, lc, hk), lambda b, t: (layer, b, 0, 0)),
                     pl.BlockSpec((None, None, lc, hk), lambda b, t: (layer, b, 0, 0)),
                     pl.BlockSpec((None, None, lc, LANE), lambda b, t: (layer, b, 0, 0)),
                     pl.BlockSpec((None, None, lc, LANE), lambda b, t: (layer, b, 0, 0))]
        args += list(cache)
    return pl.pallas_call(
        functools.partial(_dsa_body, tq=tq, lc=lc, lk=lk, lpad=lpad, chunked=chunked, qt0=qt0),
        grid=(nb, n_qt),
        in_specs=in_specs,
        out_specs=pl.BlockSpec((tq, H_B * HD_B), lambda b, t: (b * n_qt + t, 0)),
        out_shape=jax.ShapeDtypeStruct((nb * n_qt * tq, H_B * HD_B), BF16),
        scratch_shapes=[pltpu.VMEM((lpad, hk), MXU_DTYPE), pltpu.VMEM((lpad, hk), MXU_DTYPE),
                        pltpu.VMEM((lpad, LANE), MXU_DTYPE), pltpu.VMEM((lpad, LANE), MXU_DTYPE),
                        pltpu.VMEM((tq, lpad), jnp.int32), pltpu.VMEM((lpad, lpad), MXU_DTYPE)],
        compiler_params=_params(("parallel", "arbitrary")),
    )(*args)


def _branch_b_prompt(proj, tq=128, tiles_per_call=2):
    nq = SEQ // tq
    bands = [_branch_b(proj, 0, BATCH, SEQ, tq, True, qt0=q0, n_qt=tiles_per_call).reshape(BATCH, -1, H_B * HD_B)
             for q0 in range(0, nq, tiles_per_call)]
    return jnp.concatenate(bands, axis=1).reshape(T_PROMPT, H_B * HD_B)


W_SEG2_SRC, W_SEG3_SRC = 8272, 14432
W_KI_SRC, W_GL_SRC = 8192, 14416
PROJ_TN = 512


def _proj_body(x_ref, w_ref, gl_ref, o_ref, wt_ref):
    n = pl.program_id(1)
    small_tile = C_KIA // PROJ_TN
    zero_tile = small_tile + 1

    @pl.when((n != small_tile) & (n != zero_tile))
    def _regular():
        o_ref[...] = _dot_nt(x_ref[...], w_ref[0].astype(MXU_DTYPE))

    @pl.when(n == small_tile)
    def _small():
        wt_ref[...] = jnp.zeros_like(wt_ref)
        ki = w_ref[0, 0:D_I, :].astype(MXU_DTYPE)
        wt_ref[0:D_I, :] = ki
        wt_ref[2 * LANE - D_I:2 * LANE, :] = ki
        wt_ref[2 * LANE:2 * LANE + H_I, :] = w_ref[0, D_I:D_I + H_I, :].astype(MXU_DTYPE)
        wt_ref[3 * LANE:3 * LANE + GATE_RANK, :] = gl_ref[0].astype(MXU_DTYPE)
        o_ref[...] = _dot_nt(x_ref[...], wt_ref[...])

    @pl.when(n == zero_tile)
    def _zero():
        o_ref[...] = jnp.zeros_like(o_ref)


def _in_proj(xb, w_t, layer, tm=1024):
    t_dim, d = xb.shape
    tn = PROJ_TN
    small_tile = C_KIA // tn
    seg2_tile0, seg3_tile0 = C_QC // tn, C_GZ // tn

    def w_row(n):
        seg2 = W_SEG2_SRC + (n - seg2_tile0) * tn
        seg3 = W_SEG3_SRC + (n - seg3_tile0) * tn
        aligned = n * tn
        row = jnp.where(n >= seg3_tile0, seg3, jnp.where(n >= small_tile, W_KI_SRC, jnp.where(n >= seg2_tile0, seg2, aligned)))
        return pl.multiple_of(row, 16)

    return pl.pallas_call(
        _proj_body,
        grid=(t_dim // tm, N_PROJ // tn),
        in_specs=[pl.BlockSpec((tm, d), lambda m, n: (m, 0)),
                  pl.BlockSpec((pl.Element(1), pl.Element(tn), pl.Element(d)), lambda m, n: (layer, w_row(n), 0)),
                  pl.BlockSpec((pl.Element(1), pl.Element(GATE_RANK), pl.Element(d)),
                               lambda m, n: (layer, W_GL_SRC, 0))],
        out_specs=pl.BlockSpec((tm, tn), lambda m, n: (m, n)),
        out_shape=jax.ShapeDtypeStruct((t_dim, N_PROJ), F32),
        scratch_shapes=[pltpu.VMEM((tn, d), MXU_DTYPE)],
        compiler_params=_params(("parallel", "arbitrary")),
    )(xb, w_t, w_t)


def kernel(x_prompt, x_sample, cache_k, cache_v, cache_kidx, state_lru, state_conv, state_gla, w_in, conv_w, conv_b, w_rec_gate, b_rec_gate, w_in_gate, b_in_gate, lru_lambda, w_gla_gate, b_gla_gate, gla_norm_g, w_branch, b_branch_gate, w_out, ln1_g, ln1_b, ln2_g, ln2_b, w_ff_gate, w_ff_up, w_ff_down, w_router, w_exp_gate, w_exp_up, w_exp_down):
    hk = N_KV * HD_B
    x = jnp.concatenate([x_prompt.reshape(T_PROMPT, D_MODEL), x_sample.reshape(T_SAMPLE, D_MODEL)], axis=0)
    xb = x.astype(BF16)

    w_t = jnp.swapaxes(w_in, 1, 2)
    w_ffd = _cast_pad(w_ff_down, D_FF_PAD, D_MODEL)
    w_router_p = jnp.pad(w_router, ((0, 0), (0, 0), (0, LANE - N_EXPERTS)))
    wgg_p = jnp.pad(w_gla_gate, ((0, 0), (0, LANE - GATE_RANK), (0, 0)))
    wggt_p = jnp.swapaxes(wgg_p, 1, 2)
    cache_k2 = cache_k.reshape(DEPTH, DEC_BATCH, PAST_LEN, hk)
    cache_v2 = cache_v.reshape(DEPTH, DEC_BATCH, PAST_LEN, hk)
    cache_kia = jnp.pad(cache_kidx, ((0, 0), (0, 0), (0, 0), (0, LANE - D_I)))
    cache_kib = jnp.pad(cache_kidx, ((0, 0), (0, 0), (0, 0), (LANE - D_I, 0)))
    zeros_conv = jnp.zeros((BATCH, CONV_W - 1, W_A), F32)
    zeros_lru = jnp.zeros((BATCH, W_A), F32)

    outs_p = [[] for _ in range(6)]
    outs_s = [[] for _ in range(6)]
    for l in range(DEPTH):
        proj = _in_proj(xb, w_t, l)

        lru_args = (conv_w[l], conv_b[l], w_rec_gate[l], b_rec_gate[l], w_in_gate[l], b_in_gate[l], lru_lambda[l])
        a_p, lru_p = _branch_a(proj, 0, BATCH, SEQ, 256, zeros_conv, zeros_lru, *lru_args)
        a_s, lru_s = _branch_a(proj, T_PROMPT, DEC_BATCH, DEC_SEQ, DEC_SEQ, state_conv[l], state_lru[l], *lru_args)

        b_p = _branch_b_prompt(proj)
        b_s = _branch_b(proj, T_PROMPT, DEC_BATCH, DEC_SEQ, DEC_SEQ, False,
                        cache=(cache_k2, cache_v2, cache_kia, cache_kib), layer=l)

        gla_args = (wgg_p[l], wggt_p[l], b_gla_gate[l], gla_norm_g[l])
        c_p, gla_p = _branch_c(proj, 0, BATCH, SEQ, 64, None, l, *gla_args)
        c_s, gla_s = _branch_c(proj, T_PROMPT, DEC_BATCH, DEC_SEQ, DEC_SEQ, state_gla, l, *gla_args)

        branches = jnp.stack([jnp.concatenate([a_p, a_s], axis=0), jnp.concatenate([b_p, b_s], axis=0),
                              jnp.concatenate([c_p, c_s], axis=0)])
        mixed = _merge(branches, w_branch, proj, b_branch_gate[l], l)
        y = _matmul(mixed, w_out, (l,), F32, tm=1024, tn=512, tk=D_MODEL)
        x, xb = _ln_residual(x, y, ln1_g[l], ln1_b[l])

        i = l // 2
        if l % 2 == 0:
            h = _glu_dense(xb, w_ff_gate, w_ff_up, i, D_FF_PAD)
            f = _matmul(h, w_ffd, (i,), F32, tm=1024, tn=1024, tk=D_FF_PAD // 4)
        else:
            combine = _router(x, w_router_p[i])
            f = _moe_routed(xb, combine, w_exp_gate, w_exp_up, w_exp_down, i)
        x, xb = _ln_residual(x, f, ln2_g[l], ln2_b[l])

        pp, ps = proj[:T_PROMPT], proj[T_PROMPT:]
        for dst, rows, nb, seq, lru_h, gla_st in ((outs_p, pp, BATCH, SEQ, lru_p, gla_p),
                                                  (outs_s, ps, DEC_BATCH, DEC_SEQ, lru_s, gla_s)):
            dst[0].append(rows[:, C_KB:C_KB + hk].reshape(nb, seq, N_KV, HD_B))
            dst[1].append(rows[:, C_VB:C_VB + hk].reshape(nb, seq, N_KV, HD_B))
            dst[2].append(rows[:, C_KIA:C_KIA + D_I].reshape(nb, seq, D_I))
            dst[3].append(lru_h)
            dst[4].append(rows[:, C_XA:C_XA + W_A].reshape(nb, seq, W_A)[:, seq - (CONV_W - 1):])
            dst[5].append(gla_st)

    k_p, v_p, ki_p, lru_po, conv_p, gla_po = [jnp.stack(o) for o in outs_p]
    k_s, v_s, ki_s, lru_so, conv_s, gla_so = [jnp.stack(o) for o in outs_s]
    return (x[:T_PROMPT].reshape(BATCH, SEQ, D_MODEL), x[T_PROMPT:].reshape(DEC_BATCH, DEC_SEQ, D_MODEL),
            k_p, v_p, ki_p, lru_po, conv_p, gla_po, k_s, v_s, ki_s, lru_so, conv_s, gla_so)
```
